```python
import math
import jax
import jax.numpy as jnp
from jax import lax
import numpy as np

D_MODEL = 4096
BATCH = 16
SEQ = 256
DEPTH = 2
DEC_BATCH = 2
DEC_SEQ = 1024
PAST_LEN = 256

GRID_W = 64
NORM_EPS = 1e-6
ROPE_BASE = 10000.0
Q_BLOCK = 128
N_MOD = 6
A_HEADS = 16
A_QK_DIM = 64
A_V_DIM = 2 * A_QK_DIM
A_WIDTH = A_HEADS * A_V_DIM
B_WIDTH = D_MODEL - A_WIDTH
CONV_W = 3
IN0_COLS = 3 * A_WIDTH + 3 * B_WIDTH
C_HEADS = 32
Q_LORA = 1024
KV_LORA = 512
QK_NOPE = 128
QK_ROPE = 64
V_DIM = 128
DQKV_COLS = Q_LORA + KV_LORA + QK_ROPE
N_EXPERTS = 32
TOP_K = 4
D_FF = 2048
SWIGLU_ALPHA = 1.702
SWIGLU_LIMIT = 7.0

kernel_name = 'diffattn_shortconv_mla_moe_prefix_dit_step'


def _rmsnorm(x, w, eps=NORM_EPS):
    xf = x.astype(jnp.float32)
    y = xf * lax.rsqrt(jnp.mean(xf * xf, axis=-1, keepdims=True) + eps)
    return (y * w.astype(jnp.float32)).astype(x.dtype)


def _lambda_init(layer):
    return 0.8 - 0.6 * math.exp(-0.3 * layer)


def _grid_positions(length):
    rows = length // GRID_W
    t = jnp.arange(rows * GRID_W, dtype=jnp.int32)
    return t // GRID_W, t % GRID_W


def _axis_angles(pos, dim):
    inv = ROPE_BASE ** (-jnp.arange(0, dim, 2, dtype=jnp.float32) / dim)
    return pos.astype(jnp.float32)[:, None] * inv[None, :]


def _rotate(x, ang):
    xf = x.astype(jnp.float32)
    half = x.shape[-1] // 2
    x1, x2 = xf[..., :half], xf[..., half:]
    cos, sin = jnp.cos(ang), jnp.sin(ang)
    return jnp.concatenate([x1 * cos - x2 * sin, x1 * sin + x2 * cos], axis=-1).astype(x.dtype)


def _axial_rope(x, row, col):
    half = x.shape[-1] // 2
    return jnp.concatenate([_rotate(x[..., :half], _axis_angles(row, half)),
                            _rotate(x[..., half:], _axis_angles(col, half))], axis=-1)


def _rope_halves(x, row, col):
    dk = x.shape[-1] // 2
    return jnp.concatenate([_axial_rope(x[..., :dk], row, col),
                            _axial_rope(x[..., dk:], row, col)], axis=-1)


def _attend(q, k, v, scale):
    b, h, lq, dk = q.shape
    dv = v.shape[-1]
    nb = lq // Q_BLOCK
    qb = q.reshape(b, h, nb, Q_BLOCK, dk).transpose(2, 0, 1, 3, 4)

    def one(qblk):
        s = jnp.einsum('bhqd,bhkd->bhqk', qblk, k).astype(jnp.float32) * scale
        p = jax.nn.softmax(s, axis=-1)
        return jnp.einsum('bhqk,bhkd->bhqd', p.astype(v.dtype), v)

    out = lax.map(one, qb)
    return out.transpose(1, 2, 0, 3, 4).reshape(b, h, lq, dv)


def _modulation(cond, mod_w, mod_b):
    m = jax.nn.silu(cond) @ mod_w + mod_b
    m = m.reshape(cond.shape[:-1] + (1, N_MOD, D_MODEL))
    return [m[..., i, :] for i in range(N_MOD)]


def _even_mixer(h, w_in, conv_w, conv_b, lam_q1, lam_k1, lam_q2, lam_k2, subln_w, w_out,
                lambda_init, pos, ctx_k, ctx_v):
    b, l, _ = h.shape
    proj = h @ w_in
    q, k, v, gb, gc, xin = jnp.split(
        proj, [A_WIDTH, 2 * A_WIDTH, 3 * A_WIDTH, 3 * A_WIDTH + B_WIDTH, 3 * A_WIDTH + 2 * B_WIDTH], axis=-1)
    q = q.reshape(b, l, A_HEADS, 2 * A_QK_DIM).transpose(0, 2, 1, 3)
    k = k.reshape(b, l, A_HEADS, 2 * A_QK_DIM).transpose(0, 2, 1, 3)
    v = v.reshape(b, l, A_HEADS, A_V_DIM).transpose(0, 2, 1, 3)
    if pos is not None:
        q = _rope_halves(q, *pos)
        k = _rope_halves(k, *pos)
    k_all = k if ctx_k is None else jnp.concatenate([ctx_k, k], axis=2)
    v_all = v if ctx_v is None else jnp.concatenate([ctx_v, v], axis=2)
    dk = A_QK_DIM
    scale = dk ** -0.5
    lam = jnp.exp(jnp.sum(lam_q1 * lam_k1)) - jnp.exp(jnp.sum(lam_q2 * lam_k2)) + lambda_init
    o = (_attend(q[..., :dk], k_all[..., :dk], v_all, scale)
         - lam * _attend(q[..., dk:], k_all[..., dk:], v_all, scale))
    o = _rmsnorm(o, subln_w) * (1.0 - lambda_init)
    o = o.transpose(0, 2, 1, 3).reshape(b, l, A_WIDTH)
    u = gc * xin
    up = jnp.pad(u, ((0, 0), (1, 1), (0, 0)))
    y = up[:, :-2] * conv_w[0] + up[:, 1:-1] * conv_w[1] + up[:, 2:] * conv_w[2] + conv_b
    z = gb * y
    out = jnp.concatenate([o, z], axis=-1) @ w_out
    return out, k, v


def _odd_mixer(h, w_dqkv, q_norm_w, kv_norm_w, w_uq, w_ukv, w_o, pos, ctx_ckv, ctx_kpe):
    b, l, _ = h.shape
    proj = h @ w_dqkv
    cq, ckv, kpe = jnp.split(proj, [Q_LORA, Q_LORA + KV_LORA], axis=-1)
    cq = _rmsnorm(cq, q_norm_w)
    ckv = _rmsnorm(ckv, kv_norm_w)
    q = (cq @ w_uq).reshape(b, l, C_HEADS, QK_NOPE + QK_ROPE).transpose(0, 2, 1, 3)
    q_nope, q_pe = q[..., :QK_NOPE], q[..., QK_NOPE:]
    if pos is not None:
        q_pe = _axial_rope(q_pe, *pos)
        kpe = _axial_rope(kpe, *pos)
    ckv_all = ckv if ctx_ckv is None else jnp.concatenate([ctx_ckv, ckv], axis=1)
    kpe_all = kpe if ctx_kpe is None else jnp.concatenate([ctx_kpe, kpe], axis=1)
    lk = ckv_all.shape[1]
    kv = (ckv_all @ w_ukv).reshape(b, lk, C_HEADS, QK_NOPE + V_DIM).transpose(0, 2, 1, 3)
    k_nope, v = kv[..., :QK_NOPE], kv[..., QK_NOPE:]
    k_full = jnp.concatenate(
        [k_nope, jnp.broadcast_to(kpe_all[:, None], (b, C_HEADS, lk, QK_ROPE))], axis=-1)
    q_full = jnp.concatenate([q_nope, q_pe], axis=-1)
    o = _attend(q_full, k_full, v, (QK_NOPE + QK_ROPE) ** -0.5)
    out = o.transpose(0, 2, 1, 3).reshape(b, l, C_HEADS * V_DIM) @ w_o
    return out, ckv, kpe


def _moe(h, router_w, router_b, w_gu, b_gu, w_down, b_down):
    b, l, d = h.shape
    x = h.reshape(b * l, d)
    logits = (x @ router_w + router_b).astype(jnp.float32)
    top_v, top_i = lax.top_k(logits, TOP_K)
    gates = jax.nn.softmax(top_v, axis=-1)
    combine = jnp.sum(jax.nn.one_hot(top_i, N_EXPERTS, dtype=jnp.float32) * gates[..., None],
                      axis=1).astype(x.dtype)
    acc = jnp.zeros_like(x)
    for e in range(N_EXPERTS):
        hg = x @ w_gu[e] + b_gu[e]
        glu = jnp.minimum(hg[:, :D_FF], SWIGLU_LIMIT)
        lin = jnp.clip(hg[:, D_FF:], -SWIGLU_LIMIT, SWIGLU_LIMIT)
        act = glu * jax.nn.sigmoid(SWIGLU_ALPHA * glu) * (lin + 1.0)
        acc = acc + combine[:, e:e + 1] * (act @ w_down[e] + b_down[e])
    return acc.reshape(b, l, d)


def _sublayers(x, cond, norm_a_w, norm_b_w, mod_w, mod_b, mixer_fn, moe_params):
    sh_a, sc_a, g_a, sh_b, sc_b, g_b = _modulation(cond, mod_w, mod_b)
    mix_out, s0, s1 = mixer_fn(_rmsnorm(x, norm_a_w) * (1.0 + sc_a) + sh_a)
    x = x + g_a * mix_out
    x = x + g_b * _moe(_rmsnorm(x, norm_b_w) * (1.0 + sc_b) + sh_b, *moe_params)
    return x, (s0, s1)


def setup_inputs(seed: int = 0) -> dict:
    key = jax.random.key(seed)
    ks = iter(list(jax.random.split(key, 64)))
    d = D_MODEL

    def nrm(shape, scale=1.0):
        return jax.random.normal(next(ks), shape, jnp.float32) * scale

    def gain(n):
        return 1.0 + 0.02 * jax.random.normal(next(ks), (n,), jnp.float32)

    inp = {}
    inp['x_prompt'] = nrm((BATCH, SEQ, d))
    inp['x_sample'] = nrm((DEC_BATCH, DEC_SEQ, d))
    inp['cache_l0_k'] = nrm((DEC_BATCH, A_HEADS, PAST_LEN, 2 * A_QK_DIM))
    inp['cache_l0_v'] = nrm((DEC_BATCH, A_HEADS, PAST_LEN, A_V_DIM))
    inp['cache_l1_ckv'] = nrm((DEC_BATCH, PAST_LEN, KV_LORA))
    inp['cache_l1_kpe'] = nrm((DEC_BATCH, PAST_LEN, QK_ROPE))
    inp['c'] = nrm((DEC_BATCH, d))
    inp['c_ctx'] = nrm((d,))
    inp['mod0_w'] = nrm((d, N_MOD * d), 0.5 * d ** -0.5)
    inp['mod0_b'] = nrm((N_MOD * d,), 0.01)
    inp['norm0a_w'] = gain(d)
    inp['in0_w'] = nrm((d, IN0_COLS), d ** -0.5)
    inp['conv0_w'] = nrm((CONV_W, B_WIDTH), CONV_W ** -0.5)
    inp['conv0_b'] = nrm((B_WIDTH,), 0.01)
    inp['lam0_q1'] = nrm((A_QK_DIM,), 0.1)
    inp['lam0_k1'] = nrm((A_QK_DIM,), 0.1)
    inp['lam0_q2'] = nrm((A_QK_DIM,), 0.1)
    inp['lam0_k2'] = nrm((A_QK_DIM,), 0.1)
    inp['subln0_w'] = gain(A_V_DIM)
    inp['out0_w'] = nrm((A_WIDTH + B_WIDTH, d), (A_WIDTH + B_WIDTH) ** -0.5)
    inp['norm0b_w'] = gain(d)
    inp['moe0_router_w'] = nrm((d, N_EXPERTS), d ** -0.5)
    inp['moe0_router_b'] = nrm((N_EXPERTS,), 0.01)
    inp['moe0_w_gu'] = nrm((N_EXPERTS, d, 2 * D_FF), d ** -0.5)
    inp['moe0_b_gu'] = nrm((N_EXPERTS, 2 * D_FF), 0.01)
    inp['moe0_w_down'] = nrm((N_EXPERTS, D_FF, d), D_FF ** -0.5)
    inp['moe0_b_down'] = nrm((N_EXPERTS, d), 0.01)
    inp['mod1_w'] = nrm((d, N_MOD * d), 0.5 * d ** -0.5)
    inp['mod1_b'] = nrm((N_MOD * d,), 0.01)
    inp['norm1a_w'] = gain(d)
    inp['dqkv1_w'] = nrm((d, DQKV_COLS), d ** -0.5)
    inp['qnorm1_w'] = gain(Q_LORA)
    inp['kvnorm1_w'] = gain(KV_LORA)
    inp['uq1_w'] = nrm((Q_LORA, C_HEADS * (QK_NOPE + QK_ROPE)), Q_LORA ** -0.5)
    inp['ukv1_w'] = nrm((KV_LORA, C_HEADS * (QK_NOPE + V_DIM)), KV_LORA ** -0.5)
    inp['o1_w'] = nrm((C_HEADS * V_DIM, d), (C_HEADS * V_DIM) ** -0.5)
    inp['norm1b_w'] = gain(d)
    inp['moe1_router_w'] = nrm((d, N_EXPERTS), d ** -0.5)
    inp['moe1_router_b'] = nrm((N_EXPERTS,), 0.01)
    inp['moe1_w_gu'] = nrm((N_EXPERTS, d, 2 * D_FF), d ** -0.5)
    inp['moe1_b_gu'] = nrm((N_EXPERTS, 2 * D_FF), 0.01)
    inp['moe1_w_down'] = nrm((N_EXPERTS, D_FF, d), D_FF ** -0.5)
    inp['moe1_b_down'] = nrm((N_EXPERTS, d), 0.01)
    inp['final_norm_w'] = gain(d)
    return inp


def reference(x_prompt, x_sample, cache_l0_k, cache_l0_v, cache_l1_ckv, cache_l1_kpe, c, c_ctx,
              mod0_w, mod0_b, norm0a_w, in0_w, conv0_w, conv0_b, lam0_q1, lam0_k1, lam0_q2, lam0_k2,
              subln0_w, out0_w, norm0b_w, moe0_router_w, moe0_router_b, moe0_w_gu, moe0_b_gu,
              moe0_w_down, moe0_b_down,
              mod1_w, mod1_b, norm1a_w, dqkv1_w, qnorm1_w, kvnorm1_w, uq1_w, ukv1_w, o1_w, norm1b_w,
              moe1_router_w, moe1_router_b, moe1_w_gu, moe1_b_gu, moe1_w_down, moe1_b_down,
              final_norm_w):
    pos = _grid_positions(x_sample.shape[1])
    caches = ((cache_l0_k, cache_l0_v), (cache_l1_ckv, cache_l1_kpe))
    mods = ((mod0_w, mod0_b), (mod1_w, mod1_b))
    norms = ((norm0a_w, norm0b_w), (norm1a_w, norm1b_w))
    moes = ((moe0_router_w, moe0_router_b, moe0_w_gu, moe0_b_gu, moe0_w_down, moe0_b_down),
            (moe1_router_w, moe1_router_b, moe1_w_gu, moe1_b_gu, moe1_w_down, moe1_b_down))

    def make_mixer(layer, p, ctx):
        if layer % 2 == 0:
            return lambda h: _even_mixer(h, in0_w, conv0_w, conv0_b, lam0_q1, lam0_k1, lam0_q2, lam0_k2,
                                         subln0_w, out0_w, _lambda_init(layer), p, ctx[0], ctx[1])
        return lambda h: _odd_mixer(h, dqkv1_w, qnorm1_w, kvnorm1_w, uq1_w, ukv1_w, o1_w,
                                    p, ctx[0], ctx[1])

    yp, ys = x_prompt, x_sample
    states = [None] * DEPTH
    for layer in range(DEPTH):
        yp, states[layer] = _sublayers(yp, c_ctx, norms[layer][0], norms[layer][1],
                                       mods[layer][0], mods[layer][1],
                                       make_mixer(layer, None, (None, None)), moes[layer])
        ys, _ = _sublayers(ys, c, norms[layer][0], norms[layer][1],
                           mods[layer][0], mods[layer][1],
                           make_mixer(layer, pos, caches[layer]), moes[layer])
    y_prompt = _rmsnorm(yp, final_norm_w)
    y_sample = _rmsnorm(ys, final_norm_w)
    return (y_prompt, y_sample, states[0][0], states[0][1], states[1][0], states[1][1])
```

```python
import functools
import math

import jax
import jax.numpy as jnp
from jax import lax
from jax.experimental import pallas as pl
from jax.experimental.pallas import tpu as pltpu

F32 = jnp.float32
BF16 = jnp.bfloat16

D_MODEL = 4096
BATCH = 16
SEQ = 256
DEC_BATCH = 2
DEC_SEQ = 1024
PAST_LEN = 256
N_CTX = BATCH * SEQ
N_LAT = DEC_BATCH * DEC_SEQ
N_TOK = N_CTX + N_LAT
GRID_W = 64
NORM_EPS = 1e-6
ROPE_BASE = 10000.0
N_MOD = 6
A_HEADS = 16
A_QK_DIM = 64
A_V_DIM = 128
A_WIDTH = A_HEADS * A_V_DIM
B_WIDTH = D_MODEL - A_WIDTH
C_HEADS = 32
Q_LORA = 1024
KV_LORA = 512
QK_NOPE = 128
QK_ROPE = 64
V_DIM = 128
N_EXPERTS = 32
TOP_K = 4
D_FF = 2048
SWIGLU_ALPHA = 1.702
SWIGLU_LIMIT = 7.0

LANE = 128
COND_ROWS = 16
VMEM_LIMIT = 56 * 1024 * 1024
TM_MOE = 256
P_MOE = N_TOK * TOP_K + N_EXPERTS * TM_MOE
NT_MOE = P_MOE // TM_MOE

_NT_DIMS = (((1,), (1,)), ((), ()))


def _cparams(n_axes):
    return pltpu.CompilerParams(dimension_semantics=("arbitrary",) * n_axes,
                                vmem_limit_bytes=VMEM_LIMIT)


def _group_of_row(r0):
    return jnp.where(r0 < N_CTX, 0, 1 + (r0 - N_CTX) // DEC_SEQ)


def _mm_kernel(*refs, mode, tm):
    if mode == "plain":
        x_ref, w_ref, o_ref = refs
    elif mode == "bias":
        x_ref, w_ref, b_ref, o_ref = refs
    else:
        x_ref, w_ref, g_ref, r_ref, o_ref = refs
    acc = jnp.dot(x_ref[...], w_ref[...].astype(BF16), preferred_element_type=F32)
    if mode == "bias":
        acc = acc + b_ref[...]
    elif mode == "resid":
        g = _group_of_row(pl.program_id(0) * tm)
        acc = r_ref[...] + g_ref[pl.ds(g, 1), :] * acc
    o_ref[...] = acc.astype(o_ref.dtype)


def _mm(x, w, *, tm, tn, out_dtype, bias=None, resid=None, gate=None, gate_chunk=0, name):
    m, k = x.shape
    n = w.shape[1]
    assert m % tm == 0
    grid = (m // tm, pl.cdiv(n, tn))
    in_specs = [pl.BlockSpec((tm, k), lambda i, j: (i, 0)),
                pl.BlockSpec((k, tn), lambda i, j: (0, j))]
    args = [x, w]
    mode = "plain"
    if bias is not None:
        mode = "bias"
        in_specs.append(pl.BlockSpec((1, tn), lambda i, j: (0, j)))
        args.append(bias)
    elif resid is not None:
        mode = "resid"
        off = gate_chunk * (D_MODEL // tn)
        in_specs.append(pl.BlockSpec((gate.shape[0], tn), lambda i, j: (0, off + j)))
        in_specs.append(pl.BlockSpec((tm, tn), lambda i, j: (i, j)))
        args += [gate, resid]
    return pl.pallas_call(
        functools.partial(_mm_kernel, mode=mode, tm=tm),
        grid=grid, in_specs=in_specs,
        out_specs=pl.BlockSpec((tm, tn), lambda i, j: (i, j)),
        out_shape=jax.ShapeDtypeStruct((m, n), out_dtype),
        compiler_params=_cparams(2), name=name)(*args)


def _rms(x, w):
    return x * lax.rsqrt(jnp.mean(x * x, axis=-1, keepdims=True) + NORM_EPS) * w


def _norm_mod_kernel(x_ref, w_ref, sh_ref, sc_ref, *rest, tm, with_router):
    g = _group_of_row(pl.program_id(0) * tm)
    y = _rms(x_ref[...], w_ref[...])
    h = y * (1.0 + sc_ref[pl.ds(g, 1), :]) + sh_ref[pl.ds(g, 1), :]
    if with_router:
        rw_ref, rb_ref, h_ref, lg_ref = rest
        lg_ref[...] = jnp.dot(h, rw_ref[...], preferred_element_type=F32,
                              precision=lax.Precision.HIGHEST) + rb_ref[...]
    else:
        (h_ref,) = rest
    h_ref[...] = h.astype(BF16)


def _norm_mod(x, norm_w, mod, sh_chunk, sc_chunk, router_w=None, router_b=None, *, name):
    tm = 256
    with_router = router_w is not None
    in_specs = [pl.BlockSpec((tm, D_MODEL), lambda i: (i, 0)),
                pl.BlockSpec((1, D_MODEL), lambda i: (0, 0)),
                pl.BlockSpec((COND_ROWS, D_MODEL), lambda i: (0, sh_chunk)),
                pl.BlockSpec((COND_ROWS, D_MODEL), lambda i: (0, sc_chunk))]
    args = [x, norm_w.reshape(1, D_MODEL), mod, mod]
    out_specs = [pl.BlockSpec((tm, D_MODEL), lambda i: (i, 0))]
    out_shape = [jax.ShapeDtypeStruct((N_TOK, D_MODEL), BF16)]
    if with_router:
        in_specs += [pl.BlockSpec((D_MODEL, N_EXPERTS), lambda i: (0, 0)),
                     pl.BlockSpec((1, N_EXPERTS), lambda i: (0, 0))]
        args += [router_w, router_b.reshape(1, N_EXPERTS)]
        out_specs.append(pl.BlockSpec((tm, N_EXPERTS), lambda i: (i, 0)))
        out_shape.append(jax.ShapeDtypeStruct((N_TOK, N_EXPERTS), F32))
    out = pl.pallas_call(
        functools.partial(_norm_mod_kernel, tm=tm, with_router=with_router),
        grid=(N_TOK // tm,), in_specs=in_specs, out_specs=out_specs, out_shape=out_shape,
        compiler_params=_cparams(1), name=name)(*args)
    return out if with_router else out[0]


def _rms_kernel(x_ref, w_ref, o_ref):
    o_ref[...] = _rms(x_ref[...], w_ref[...]).astype(o_ref.dtype)


def _rmsnorm_cols(x, w, col_block, width, out_dtype, *, name):
    tm = 256
    m = x.shape[0]
    return pl.pallas_call(
        _rms_kernel, grid=(m // tm,),
        in_specs=[pl.BlockSpec((tm, width), lambda i: (i, col_block)),
                  pl.BlockSpec((1, width), lambda i: (0, 0))],
        out_specs=pl.BlockSpec((tm, width), lambda i: (i, 0)),
        out_shape=jax.ShapeDtypeStruct((m, width), out_dtype),
        compiler_params=_cparams(1), name=name)(x, w.reshape(1, width))


def _rope_tables():
    t = jnp.arange(DEC_SEQ, dtype=jnp.int32)
    row = (t // GRID_W).astype(F32)[:, None]
    col = (t % GRID_W).astype(F32)[:, None]
    half = QK_ROPE // 2
    inv = ROPE_BASE ** (-jnp.arange(0, half, 2, dtype=F32) / half)[None, :]
    ar, ac = row * inv, col * inv
    cos64 = jnp.concatenate([jnp.cos(ar), jnp.cos(ar), jnp.cos(ac), jnp.cos(ac)], axis=-1)
    sin64 = jnp.concatenate([-jnp.sin(ar), jnp.sin(ar), -jnp.sin(ac), jnp.sin(ac)], axis=-1)
    return jnp.tile(cos64, (1, 2)), jnp.tile(sin64, (1, 2))


def _swap16(x):
    lane = lax.broadcasted_iota(jnp.int32, x.shape, x.ndim - 1)
    return jnp.where((lane & 16) == 0,
                     pltpu.roll(x, x.shape[-1] - 16, x.ndim - 1),
                     pltpu.roll(x, 16, x.ndim - 1))


def _rope_kernel(x_ref, cos_ref, sin_ref, o_ref, *, n_chunks):
    cos = cos_ref[...]
    sin = sin_ref[...]
    for c in range(n_chunks):
        x = x_ref[:, c * LANE:(c + 1) * LANE]
        o_ref[:, c * LANE:(c + 1) * LANE] = (x * cos + _swap16(x) * sin).astype(o_ref.dtype)


def _rope_latent(x, cos, sin, col0, width, *, name):
    tm, tc = 256, 2048
    assert width % tc == 0 and col0 % tc == 0
    r0, c0 = N_CTX // tm, col0 // tc
    per_seq = DEC_SEQ // tm
    return pl.pallas_call(
        functools.partial(_rope_kernel, n_chunks=tc // LANE),
        grid=(N_LAT // tm, width // tc),
        in_specs=[pl.BlockSpec((tm, tc), lambda i, j: (r0 + i, c0 + j)),
                  pl.BlockSpec((tm, LANE), lambda i, j: (i % per_seq, 0)),
                  pl.BlockSpec((tm, LANE), lambda i, j: (i % per_seq, 0))],
        out_specs=pl.BlockSpec((tm, tc), lambda i, j: (i, j)),
        out_shape=jax.ShapeDtypeStruct((N_LAT, width), BF16),
        compiler_params=_cparams(2), name=name)(x, cos, sin)


def _rope_kpe_kernel(x_ref, cos_ref, sin_ref, o_ref):
    x = x_ref[...]
    o_ref[...] = x * cos_ref[...] + _swap16(x) * sin_ref[...]


def _rope_kpe(kpe_pad, cos, sin, *, name):
    tm = 256
    per_seq = DEC_SEQ // tm
    return pl.pallas_call(
        _rope_kpe_kernel, grid=(N_LAT // tm,),
        in_specs=[pl.BlockSpec((tm, LANE), lambda i: (i, 0)),
                  pl.BlockSpec((tm, LANE), lambda i: (i % per_seq, 0)),
                  pl.BlockSpec((tm, LANE), lambda i: (i % per_seq, 0))],
        out_specs=pl.BlockSpec((tm, LANE), lambda i: (i, 0)),
        out_shape=jax.ShapeDtypeStruct((N_LAT, LANE), F32),
        compiler_params=_cparams(1), name=name)(kpe_pad, cos, sin)


def _softmax_pv(s_parts, v_parts):
    m = s_parts[0].max(axis=-1, keepdims=True)
    for s in s_parts[1:]:
        m = jnp.maximum(m, s.max(axis=-1, keepdims=True))
    acc, l = None, None
    for s, v in zip(s_parts, v_parts):
        p = jnp.exp(s - m)
        ps = p.sum(axis=-1, keepdims=True)
        pv = jnp.dot(p.astype(BF16), v, preferred_element_type=F32)
        acc = pv if acc is None else acc + pv
        l = ps if l is None else l + ps
    return acc / l


def _diff_lambda(lam_ref, lambda_init):
    lv = lam_ref[...]
    s1 = jnp.sum(lv[0:1] * lv[1:2], axis=-1, keepdims=True)
    s2 = jnp.sum(lv[2:3] * lv[3:4], axis=-1, keepdims=True)
    return jnp.exp(s1) - jnp.exp(s2) + lambda_init


def _diff_finish(o1, o2, lam, subw, lambda_init):
    o = o1 - lam * o2
    return _rms(o, subw) * (1.0 - lambda_init)


def _diff_ctx_kernel(lam_ref, q_ref, k_ref, v_ref, subw_ref, o_ref, ks_ref, vs_ref, *, lambda_init):
    dk = A_QK_DIM
    k = k_ref[...]
    v = v_ref[...]
    ks_ref[...] = k
    vs_ref[...] = v
    q = (q_ref[...] * (dk ** -0.5)).astype(BF16)
    kb = k.astype(BF16)
    vb = v.astype(BF16)
    outs = []
    for lo in (0, dk):
        s = lax.dot_general(q[:, lo:lo + dk], kb[:, lo:lo + dk], _NT_DIMS, preferred_element_type=F32)
        outs.append(_softmax_pv([s], [vb]))
    lam = _diff_lambda(lam_ref, lambda_init)
    o_ref[...] = _diff_finish(outs[0], outs[1], lam, subw_ref[...], lambda_init).astype(BF16)


def _diff_ctx(proj, lamv, subw, lambda_init):
    kc, vc = A_WIDTH // LANE, 2 * A_WIDTH // LANE
    blk = lambda off: pl.BlockSpec((SEQ, LANE), lambda b, h: (b, off + h))
    st = pl.BlockSpec((None, None, SEQ, LANE), lambda b, h: (b, h, 0, 0))
    return pl.pallas_call(
        functools.partial(_diff_ctx_kernel, lambda_init=lambda_init),
        grid=(BATCH, A_HEADS),
        in_specs=[pl.BlockSpec((4, A_QK_DIM), lambda b, h: (0, 0)),
                  blk(0), blk(kc), blk(vc),
                  pl.BlockSpec((1, A_V_DIM), lambda b, h: (0, 0))],
        out_specs=[pl.BlockSpec((SEQ, LANE), lambda b, h: (b, h)), st, st],
        out_shape=[jax.ShapeDtypeStruct((N_CTX, A_WIDTH), BF16),
                   jax.ShapeDtypeStruct((BATCH, A_HEADS, SEQ, 2 * A_QK_DIM), F32),
                   jax.ShapeDtypeStruct((BATCH, A_HEADS, SEQ, A_V_DIM), F32)],
        compiler_params=_cparams(2), name="diff_attn_ctx")(lamv, proj, proj, proj, subw)


def _diff_lat_kernel(lam_ref, q_ref, k_ref, v_ref, ck_ref, cv_ref, subw_ref, o_ref, *, lambda_init):
    dk = A_QK_DIM
    q = q_ref[...] * (dk ** -0.5)
    kb = k_ref[...]
    ckb = ck_ref[...].astype(BF16)
    vb = v_ref[...].astype(BF16)
    cvb = cv_ref[...].astype(BF16)
    outs = []
    for lo in (0, dk):
        qh = q[:, lo:lo + dk]
        sc = lax.dot_general(qh, ckb[:, lo:lo + dk], _NT_DIMS, preferred_element_type=F32)
        sn = lax.dot_general(qh, kb[:, lo:lo + dk], _NT_DIMS, preferred_element_type=F32)
        outs.append(_softmax_pv([sc, sn], [cvb, vb]))
    lam = _diff_lambda(lam_ref, lambda_init)
    o_ref[...] = _diff_finish(outs[0], outs[1], lam, subw_ref[...], lambda_init).astype(BF16)


def _diff_lat(qk_rot, proj, cache_k, cache_v, lamv, subw, lambda_init):
    tq = 256
    nq = DEC_SEQ // tq
    kc, vc = A_WIDTH // LANE, 2 * A_WIDTH // LANE
    v_row0 = N_CTX // DEC_SEQ
    cache = pl.BlockSpec((None, None, PAST_LEN, LANE), lambda b, h, i: (b, h, 0, 0))
    return pl.pallas_call(
        functools.partial(_diff_lat_kernel, lambda_init=lambda_init),
        grid=(DEC_BATCH, A_HEADS, nq),
        in_specs=[pl.BlockSpec((4, A_QK_DIM), lambda b, h, i: (0, 0)),
                  pl.BlockSpec((tq, LANE), lambda b, h, i: (b * nq + i, h)),
                  pl.BlockSpec((DEC_SEQ, LANE), lambda b, h, i: (b, kc + h)),
                  pl.BlockSpec((DEC_SEQ, LANE), lambda b, h, i: (v_row0 + b, vc + h)),
                  cache, cache,
                  pl.BlockSpec((1, A_V_DIM), lambda b, h, i: (0, 0))],
        out_specs=pl.BlockSpec((tq, LANE), lambda b, h, i: (b * nq + i, h)),
        out_shape=jax.ShapeDtypeStruct((N_LAT, A_WIDTH), BF16),
        compiler_params=_cparams(3), name="diff_attn_lat")(
            lamv, qk_rot, qk_rot, proj, cache_k, cache_v, subw)


def _mla_kernel(qn_ref, qp_ref, kv_ref, kpe_ref, o_ref):
    scale = (QK_NOPE + QK_ROPE) ** -0.5
    qn = (qn_ref[...] * scale).astype(BF16)
    qp = (qp_ref[...].astype(F32) * scale).astype(BF16)
    kv = kv_ref[...]
    kpe = kpe_ref[...].astype(BF16)
    for hh in range(2):
        kn = kv[:, hh * 2 * LANE:hh * 2 * LANE + LANE]
        v = kv[:, hh * 2 * LANE + LANE:(hh + 1) * 2 * LANE]
        s = lax.dot_general(qn[:, hh * LANE:(hh + 1) * LANE], kn, _NT_DIMS, preferred_element_type=F32)
        s = s + lax.dot_general(qp[:, hh * QK_ROPE:(hh + 1) * QK_ROPE], kpe, _NT_DIMS,
                                preferred_element_type=F32)
        o_ref[:, hh * LANE:(hh + 1) * LANE] = _softmax_pv([s], [v]).astype(BF16)


def _mla_ctx(q_all, kv_ctx, kpe_all):
    hp = C_HEADS // 2
    pe0 = C_HEADS * QK_NOPE // LANE
    return pl.pallas_call(
        _mla_kernel, grid=(BATCH, hp),
        in_specs=[pl.BlockSpec((SEQ, 2 * LANE), lambda b, h: (b, h)),
                  pl.BlockSpec((SEQ, LANE), lambda b, h: (b, pe0 + h)),
                  pl.BlockSpec((SEQ, 4 * LANE), lambda b, h: (b, h)),
                  pl.BlockSpec((SEQ, QK_ROPE), lambda b, h: (b, 0))],
        out_specs=pl.BlockSpec((SEQ, 2 * LANE), lambda b, h: (b, h)),
        out_shape=jax.ShapeDtypeStruct((N_CTX, C_HEADS * V_DIM), BF16),
        compiler_params=_cparams(2), name="mla_ctx")(q_all, q_all, kv_ctx, kpe_all)


def _mla_lat(q_all, qpe_rot, kv_lat, kpe_lat):
    tq = 256
    nq = DEC_SEQ // tq
    hp = C_HEADS // 2
    lk = PAST_LEN + DEC_SEQ
    r0 = N_CTX // tq
    return pl.pallas_call(
        _mla_kernel, grid=(DEC_BATCH, hp, nq),
        in_specs=[pl.BlockSpec((tq, 2 * LANE), lambda b, h, i: (r0 + b * nq + i, h)),
                  pl.BlockSpec((tq, LANE), lambda b, h, i: (b * nq + i, h)),
                  pl.BlockSpec((lk, 4 * LANE), lambda b, h, i: (b, h)),
                  pl.BlockSpec((None, lk, QK_ROPE), lambda b, h, i: (b, 0, 0))],
        out_specs=pl.BlockSpec((tq, 2 * LANE), lambda b, h, i: (b * nq + i, h)),
        out_shape=jax.ShapeDtypeStruct((N_LAT, C_HEADS * V_DIM), BF16),
        compiler_params=_cparams(3), name="mla_lat")(q_all, qpe_rot, kv_lat, kpe_lat)


def _conv_kernel(gb_ref, gc_ref, x_ref, cw_ref, cb_ref, z_ref, *, seq):
    u = gc_ref[...] * x_ref[...]
    row = lax.broadcasted_iota(jnp.int32, u.shape, 0)
    prev = jnp.where(row == 0, 0.0, pltpu.roll(u, 1, 0))
    nxt = jnp.where(row == seq - 1, 0.0, pltpu.roll(u, seq - 1, 0))
    cw = cw_ref[...]
    y = prev * cw[0:1] + u * cw[1:2] + nxt * cw[2:3] + cb_ref[...]
    z_ref[...] = (gb_ref[...] * y).astype(BF16)


def _short_conv(proj, conv_w, conv_b, row0, n_rows, seq, *, name):
    tc = 512
    base = 3 * A_WIDTH // tc
    nb = B_WIDTH // tc
    r0 = row0 // seq
    blk = lambda off: pl.BlockSpec((seq, tc), lambda i, j: (r0 + i, base + off * nb + j))
    return pl.pallas_call(
        functools.partial(_conv_kernel, seq=seq),
        grid=(n_rows // seq, nb),
        in_specs=[blk(0), blk(1), blk(2),
                  pl.BlockSpec((3, tc), lambda i, j: (0, j)),
                  pl.BlockSpec((1, tc), lambda i, j: (0, j))],
        out_specs=pl.BlockSpec((seq, tc), lambda i, j: (i, j)),
        out_shape=jax.ShapeDtypeStruct((n_rows, B_WIDTH), BF16),
        compiler_params=_cparams(2), name=name)(proj, proj, proj, conv_w, conv_b.reshape(1, B_WIDTH))


def _moe_gu_kernel(te_ref, tf_ref, tv_ref, x_ref, wg_ref, wl_ref, bg_ref, bl_ref, a_ref, wg_s, wl_s):
    i = pl.program_id(1)

    @pl.when(tf_ref[i] == 1)
    def _():
        wg_s[...] = wg_ref[...].astype(BF16)
        wl_s[...] = wl_ref[...].astype(BF16)

    @pl.when(tv_ref[i] == 1)
    def _():
        x = x_ref[...]
        g = jnp.dot(x, wg_s[...], preferred_element_type=F32) + bg_ref[...]
        l = jnp.dot(x, wl_s[...], preferred_element_type=F32) + bl_ref[...]
        glu = jnp.minimum(g, SWIGLU_LIMIT)
        lin = jnp.clip(l, -SWIGLU_LIMIT, SWIGLU_LIMIT)
        a_ref[...] = (glu * jax.nn.sigmoid(SWIGLU_ALPHA * glu) * (lin + 1.0)).astype(BF16)

    @pl.when(tv_ref[i] == 0)
    def _():
        a_ref[...] = jnp.zeros_like(a_ref)


def _moe_down_kernel(te_ref, tf_ref, tv_ref, a_ref, w_ref, b_ref, g_ref, y_ref, w_s):
    i = pl.program_id(1)

    @pl.when(tf_ref[i] == 1)
    def _():
        w_s[...] = w_ref[...].astype(BF16)

    @pl.when(tv_ref[i] == 1)
    def _():
        y = jnp.dot(a_ref[...], w_s[...], preferred_element_type=F32) + b_ref[...]
        y_ref[...] = (g_ref[...] * y).astype(y_ref.dtype)

    @pl.when(tv_ref[i] == 0)
    def _():
        y_ref[...] = jnp.zeros_like(y_ref)


def _moe_experts(xs, row_gate, tile_e, tile_first, tile_valid, w_gu, b_gu, w_down, b_down):
    ta = 512
    nj = D_FF // ta
    act = pl.pallas_call(
        _moe_gu_kernel,
        grid_spec=pltpu.PrefetchScalarGridSpec(
            num_scalar_prefetch=3, grid=(nj, NT_MOE),
            in_specs=[pl.BlockSpec((TM_MOE, D_MODEL), lambda j, i, te, tf, tv: (i, 0)),
                      pl.BlockSpec((None, D_MODEL, ta), lambda j, i, te, tf, tv: (te[i], 0, j)),
                      pl.BlockSpec((None, D_MODEL, ta), lambda j, i, te, tf, tv: (te[i], 0, nj + j)),
                      pl.BlockSpec((None, 1, ta), lambda j, i, te, tf, tv: (te[i], 0, j)),
                      pl.BlockSpec((None, 1, ta), lambda j, i, te, tf, tv: (te[i], 0, nj + j))],
            out_specs=pl.BlockSpec((TM_MOE, ta), lambda j, i, te, tf, tv: (i, j)),
            scratch_shapes=[pltpu.VMEM((D_MODEL, ta), BF16), pltpu.VMEM((D_MODEL, ta), BF16)]),
        out_shape=jax.ShapeDtypeStruct((P_MOE, D_FF), BF16),
        compiler_params=_cparams(2), name="moe_gate_up")(
            tile_e, tile_first, tile_valid, xs, w_gu, w_gu,
            b_gu.reshape(N_EXPERTS, 1, 2 * D_FF), b_gu.reshape(N_EXPERTS, 1, 2 * D_FF))
    tn = 1024
    return pl.pallas_call(
        _moe_down_kernel,
        grid_spec=pltpu.PrefetchScalarGridSpec(
            num_scalar_prefetch=3, grid=(D_MODEL // tn, NT_MOE),
            in_specs=[pl.BlockSpec((TM_MOE, D_FF), lambda j, i, te, tf, tv: (i, 0)),
                      pl.BlockSpec((None, D_FF, tn), lambda j, i, te, tf, tv: (te[i], 0, j)),
                      pl.BlockSpec((None, 1, tn), lambda j, i, te, tf, tv: (te[i], 0, j)),
                      pl.BlockSpec((TM_MOE, 1), lambda j, i, te, tf, tv: (i, 0))],
            out_specs=pl.BlockSpec((TM_MOE, tn), lambda j, i, te, tf, tv: (i, j)),
            scratch_shapes=[pltpu.VMEM((D_FF, tn), BF16)]),
        out_shape=jax.ShapeDtypeStruct((P_MOE, D_MODEL), F32),
        compiler_params=_cparams(2), name="moe_down")(
            tile_e, tile_first, tile_valid, act, w_down, b_down.reshape(N_EXPERTS, 1, D_MODEL), row_gate)


def _moe_routing(logits):
    top_v, top_i = lax.top_k(logits, TOP_K)
    gates = jax.nn.softmax(top_v, axis=-1)
    e_flat = top_i.reshape(-1)
    onehot = (e_flat[:, None] == jnp.arange(N_EXPERTS, dtype=jnp.int32)[None, :]).astype(jnp.int32)
    counts = onehot.sum(axis=0)
    rank = jnp.take_along_axis(jnp.cumsum(onehot, axis=0) - onehot, e_flat[:, None], axis=1)[:, 0]
    padded = ((counts + TM_MOE - 1) // TM_MOE) * TM_MOE
    ends = jnp.cumsum(padded)
    starts = ends - padded
    pos = starts[e_flat] + rank
    row_token = jnp.zeros((P_MOE,), jnp.int32).at[pos].set(
        jnp.arange(N_TOK * TOP_K, dtype=jnp.int32) // TOP_K)
    row_gate = jnp.zeros((P_MOE,), F32).at[pos].set(gates.reshape(-1))
    tile_row = jnp.arange(NT_MOE, dtype=jnp.int32) * TM_MOE
    tile_valid = (tile_row < ends[-1]).astype(jnp.int32)
    last_row = jnp.maximum(ends[-1] - TM_MOE, 0)
    tile_e = jnp.searchsorted(ends, jnp.minimum(tile_row, last_row), side="right").astype(jnp.int32)
    tile_e = jnp.minimum(tile_e, N_EXPERTS - 1)
    tile_first = ((tile_row == starts[tile_e]) & (tile_valid == 1)).astype(jnp.int32)
    tile_first = tile_first.at[0].set(1)
    return pos.reshape(N_TOK, TOP_K), row_token, row_gate.reshape(P_MOE, 1), tile_e, tile_first, tile_valid


def _moe(h, logits, w_gu, b_gu, w_down, b_down):
    pos, row_token, row_gate, tile_e, tile_first, tile_valid = _moe_routing(logits)
    xs = jnp.take(h, row_token, axis=0)
    y = _moe_experts(xs, row_gate, tile_e, tile_first, tile_valid, w_gu, b_gu, w_down, b_down)
    return jnp.take(y, pos, axis=0).sum(axis=1)


def _combine_kernel(x_ref, m_ref, g_ref, o_ref, *, tm):
    g = _group_of_row(pl.program_id(0) * tm)
    o_ref[...] = x_ref[...] + g_ref[pl.ds(g, 1), :] * m_ref[...]


def _gated_add(x, m, mod, chunk, *, name):
    tm = 256
    return pl.pallas_call(
        functools.partial(_combine_kernel, tm=tm), grid=(N_TOK // tm,),
        in_specs=[pl.BlockSpec((tm, D_MODEL), lambda i: (i, 0)),
                  pl.BlockSpec((tm, D_MODEL), lambda i: (i, 0)),
                  pl.BlockSpec((COND_ROWS, D_MODEL), lambda i: (0, chunk))],
        out_specs=pl.BlockSpec((tm, D_MODEL), lambda i: (i, 0)),
        out_shape=jax.ShapeDtypeStruct((N_TOK, D_MODEL), F32),
        compiler_params=_cparams(1), name=name)(x, m, mod)


def _lambda_init(layer):
    return 0.8 - 0.6 * math.exp(-0.3 * layer)


def kernel(x_prompt, x_sample, cache_l0_k, cache_l0_v, cache_l1_ckv, cache_l1_kpe, c, c_ctx, mod0_w, mod0_b, norm0a_w, in0_w, conv0_w, conv0_b, lam0_q1, lam0_k1, lam0_q2, lam0_k2, subln0_w, out0_w, norm0b_w, moe0_router_w, moe0_router_b, moe0_w_gu, moe0_b_gu, moe0_w_down, moe0_b_down, mod1_w, mod1_b, norm1a_w, dqkv1_w, qnorm1_w, kvnorm1_w, uq1_w, ukv1_w, o1_w, norm1b_w, moe1_router_w, moe1_router_b, moe1_w_gu, moe1_b_gu, moe1_w_down, moe1_b_down, final_norm_w):
    x = jnp.concatenate([x_prompt.reshape(N_CTX, D_MODEL), x_sample.reshape(N_LAT, D_MODEL)], axis=0)
    cond = jnp.zeros((COND_ROWS, D_MODEL), F32).at[0].set(c_ctx).at[1:1 + DEC_BATCH].set(c)
    scond = jax.nn.silu(cond).astype(BF16)
    cos, sin = _rope_tables()

    mod = _mm(scond, mod0_w, tm=COND_ROWS, tn=512, out_dtype=F32,
              bias=mod0_b.reshape(1, -1), name="mod0")
    h = _norm_mod(x, norm0a_w, mod, 0, 1, name="norm0a")
    proj = _mm(h, in0_w, tm=1024, tn=512, out_dtype=F32, name="in0_proj")
    qk_rot = _rope_latent(proj, cos, sin, 0, 2 * A_WIDTH, name="rope0")
    lamv = jnp.stack([lam0_q1, lam0_k1, lam0_q2, lam0_k2])
    subw = subln0_w.reshape(1, A_V_DIM)
    li = _lambda_init(0)
    o_ctx, state_l0_k, state_l0_v = _diff_ctx(proj, lamv, subw, li)
    o_lat = _diff_lat(qk_rot, proj, cache_l0_k, cache_l0_v, lamv, subw, li)
    z_ctx = _short_conv(proj, conv0_w, conv0_b, 0, N_CTX, SEQ, name="conv_ctx")
    z_lat = _short_conv(proj, conv0_w, conv0_b, N_CTX, N_LAT, DEC_SEQ, name="conv_lat")
    mix = jnp.concatenate([jnp.concatenate([o_ctx, o_lat], axis=0),
                           jnp.concatenate([z_ctx, z_lat], axis=0)], axis=1)
    x = _mm(mix, out0_w, tm=1024, tn=512, out_dtype=F32, resid=x, gate=mod, gate_chunk=2, name="out0_proj")
    h, logits = _norm_mod(x, norm0b_w, mod, 3, 4, moe0_router_w, moe0_router_b, name="norm0b")
    m = _moe(h, logits, moe0_w_gu, moe0_b_gu, moe0_w_down, moe0_b_down)
    x = _gated_add(x, m, mod, 5, name="moe0_add")

    mod = _mm(scond, mod1_w, tm=COND_ROWS, tn=512, out_dtype=F32,
              bias=mod1_b.reshape(1, -1), name="mod1")
    h = _norm_mod(x, norm1a_w, mod, 0, 1, name="norm1a")
    proj = _mm(h, dqkv1_w, tm=1024, tn=256, out_dtype=F32, name="dqkv_proj")
    cq = _rmsnorm_cols(proj, qnorm1_w, 0, Q_LORA, BF16, name="q_norm")
    ckv = _rmsnorm_cols(proj, kvnorm1_w, Q_LORA // KV_LORA, KV_LORA, F32, name="kv_norm")
    kpe = proj[:, Q_LORA + KV_LORA:]
    uq = uq1_w.reshape(Q_LORA, C_HEADS, QK_NOPE + QK_ROPE)
    uq = jnp.concatenate([uq[:, :, :QK_NOPE].reshape(Q_LORA, -1), uq[:, :, QK_NOPE:].reshape(Q_LORA, -1)], axis=1)
    q_all = _mm(cq, uq, tm=1024, tn=512, out_dtype=F32, name="uq_proj")
    qpe_rot = _rope_latent(q_all, cos, sin, C_HEADS * QK_NOPE, C_HEADS * QK_ROPE, name="rope1_q")
    kpe_lat = kpe[N_CTX:]
    kpe_rot = _rope_kpe(jnp.concatenate([kpe_lat, kpe_lat], axis=1), cos, sin, name="rope1_k")[:, :QK_ROPE]
    kpe_lat_all = jnp.concatenate([cache_l1_kpe, kpe_rot.reshape(DEC_BATCH, DEC_SEQ, QK_ROPE)], axis=1)
    ckv_lat_all = jnp.concatenate([cache_l1_ckv, ckv[N_CTX:].reshape(DEC_BATCH, DEC_SEQ, KV_LORA)], axis=1)
    kv_ctx = _mm(ckv[:N_CTX].astype(BF16), ukv1_w, tm=1024, tn=512, out_dtype=BF16, name="ukv_ctx")
    kv_lat = _mm(ckv_lat_all.reshape(-1, KV_LORA).astype(BF16), ukv1_w, tm=1280, tn=512,
                 out_dtype=BF16, name="ukv_lat")
    o_ctx = _mla_ctx(q_all, kv_ctx, kpe)
    o_lat = _mla_lat(q_all, qpe_rot, kv_lat, kpe_lat_all)
    mix = jnp.concatenate([o_ctx, o_lat], axis=0)
    x = _mm(mix, o1_w, tm=1024, tn=512, out_dtype=F32, resid=x, gate=mod, gate_chunk=2, name="o1_proj")
    h, logits = _norm_mod(x, norm1b_w, mod, 3, 4, moe1_router_w, moe1_router_b, name="norm1b")
    m = _moe(h, logits, moe1_w_gu, moe1_b_gu, moe1_w_down, moe1_b_down)
    x = _gated_add(x, m, mod, 5, name="moe1_add")

    y = _rmsnorm_cols(x, final_norm_w, 0, D_MODEL, F32, name="final_norm")
    state_l1_ckv = ckv[:N_CTX].reshape(BATCH, SEQ, KV_LORA)
    state_l1_kpe = kpe[:N_CTX].reshape(BATCH, SEQ, QK_ROPE)
    return (y[:N_CTX].reshape(BATCH, SEQ, D_MODEL), y[N_CTX:].reshape(DEC_BATCH, DEC_SEQ, D_MODEL),
            state_l0_k, state_l0_v, state_l1_ckv, state_l1_kpe)
```

```python
import functools
import math

import jax
import jax.numpy as jnp
from jax import lax
from jax.experimental import pallas as pl
from jax.experimental.pallas import tpu as pltpu

F32 = jnp.float32
BF16 = jnp.bfloat16

D_MODEL = 4096
BATCH = 16
SEQ = 256
DEC_BATCH = 2
DEC_SEQ = 1024
PAST_LEN = 256
N_CTX = BATCH * SEQ
N_LAT = DEC_BATCH * DEC_SEQ
N_TOK = N_CTX + N_LAT
GRID_W = 64
NORM_EPS = 1e-6
ROPE_BASE = 10000.0
N_MOD = 6
A_HEADS = 16
A_QK_DIM = 64
A_V_DIM = 128
A_WIDTH = A_HEADS * A_V_DIM
B_WIDTH = D_MODEL - A_WIDTH
C_HEADS = 32
Q_LORA = 1024
KV_LORA = 512
QK_NOPE = 128
QK_ROPE = 64
V_DIM = 128
N_EXPERTS = 32
TOP_K = 4
D_FF = 2048
SWIGLU_ALPHA = 1.702
SWIGLU_LIMIT = 7.0

LANE = 128
COND_ROWS = 16
VMEM_LIMIT = 56 * 1024 * 1024
MOE_VMEM_LIMIT = 60 * 1024 * 1024
TM_MOE = 256
R_CAP = 1024
TILES_PER_CHUNK = R_CAP // TM_MOE
N_CHUNKS = N_EXPERTS + (N_TOK * TOP_K) // R_CAP
PX_MOE = N_TOK * TOP_K + N_EXPERTS * TM_MOE + R_CAP
PY_MOE = N_CHUNKS * R_CAP
HALF_D = D_MODEL // 2
HI_MASK = -65536

_NT_DIMS = (((1,), (1,)), ((), ()))


def _cparams(n_axes):
    return pltpu.CompilerParams(dimension_semantics=("arbitrary",) * n_axes,
                                vmem_limit_bytes=VMEM_LIMIT)


def _group_of_row(r0):
    return jnp.where(r0 < N_CTX, 0, 1 + (r0 - N_CTX) // DEC_SEQ)


def _mm_kernel(*refs, mode, tm):
    if mode == "plain":
        x_ref, w_ref, o_ref = refs
    elif mode == "bias":
        x_ref, w_ref, b_ref, o_ref = refs
    else:
        x_ref, w_ref, g_ref, r_ref, o_ref = refs
    acc = jnp.dot(x_ref[...], w_ref[...].astype(BF16), preferred_element_type=F32)
    if mode == "bias":
        acc = acc + b_ref[...]
    elif mode == "resid":
        g = _group_of_row(pl.program_id(0) * tm)
        acc = r_ref[...] + g_ref[pl.ds(g, 1), :] * acc
    o_ref[...] = acc.astype(o_ref.dtype)


def _mm(x, w, *, tm, tn, out_dtype, bias=None, resid=None, gate=None, gate_chunk=0, name):
    m, k = x.shape
    n = w.shape[1]
    assert m % tm == 0
    grid = (m // tm, pl.cdiv(n, tn))
    in_specs = [pl.BlockSpec((tm, k), lambda i, j: (i, 0)),
                pl.BlockSpec((k, tn), lambda i, j: (0, j))]
    args = [x, w]
    mode = "plain"
    if bias is not None:
        mode = "bias"
        in_specs.append(pl.BlockSpec((1, tn), lambda i, j: (0, j)))
        args.append(bias)
    elif resid is not None:
        mode = "resid"
        off = gate_chunk * (D_MODEL // tn)
        in_specs.append(pl.BlockSpec((gate.shape[0], tn), lambda i, j: (0, off + j)))
        in_specs.append(pl.BlockSpec((tm, tn), lambda i, j: (i, j)))
        args += [gate, resid]
    return pl.pallas_call(
        functools.partial(_mm_kernel, mode=mode, tm=tm),
        grid=grid, in_specs=in_specs,
        out_specs=pl.BlockSpec((tm, tn), lambda i, j: (i, j)),
        out_shape=jax.ShapeDtypeStruct((m, n), out_dtype),
        compiler_params=_cparams(2), name=name)(*args)


def _rms(x, w):
    return x * lax.rsqrt(jnp.mean(x * x, axis=-1, keepdims=True) + NORM_EPS) * w


def _pack_bf16_pair(lo, hi):
    lo_b = lax.bitcast_convert_type(lo.astype(BF16).astype(F32), jnp.int32)
    hi_b = lax.bitcast_convert_type(hi.astype(BF16).astype(F32), jnp.int32)
    return hi_b | ((lo_b >> 16) & 0xFFFF)


def _unpack_bf16_pair(w):
    return (lax.bitcast_convert_type(w << 16, F32), lax.bitcast_convert_type(w & HI_MASK, F32))


def _top_k_gates(logits):
    lane = lax.broadcasted_iota(jnp.int32, logits.shape, 1)
    vals, idxs = [], []
    for _ in range(TOP_K):
        m = logits.max(axis=-1, keepdims=True)
        idx = jnp.min(jnp.where(logits == m, lane, N_EXPERTS), axis=-1, keepdims=True)
        vals.append(m)
        idxs.append(idx)
        logits = jnp.where(lane == idx, -jnp.inf, logits)
    exps = [jnp.exp(v - vals[0]) for v in vals]
    denom = exps[0]
    for e in exps[1:]:
        denom = denom + e
    return idxs, [e / denom for e in exps]


def _norm_mod_kernel(x_ref, w_ref, sh_ref, sc_ref, *rest, tm, with_router):
    g = _group_of_row(pl.program_id(0) * tm)
    y = _rms(x_ref[...], w_ref[...])
    h = y * (1.0 + sc_ref[pl.ds(g, 1), :]) + sh_ref[pl.ds(g, 1), :]
    if not with_router:
        (h_ref,) = rest
        h_ref[...] = h.astype(BF16)
        return
    rw_ref, rb_ref, hp_ref, ti_ref, gt_ref = rest
    logits = jnp.dot(h, rw_ref[...], preferred_element_type=F32,
                     precision=lax.Precision.HIGHEST) + rb_ref[...]
    idxs, gates = _top_k_gates(logits)
    for k in range(TOP_K):
        ti_ref[:, k:k + 1] = idxs[k]
        gt_ref[:, k:k + 1] = gates[k]
    hp_ref[...] = _pack_bf16_pair(h[:, :HALF_D], h[:, HALF_D:])


def _norm_mod(x, norm_w, mod, sh_chunk, sc_chunk, router_w=None, router_b=None, *, name):
    tm = 256
    with_router = router_w is not None
    in_specs = [pl.BlockSpec((tm, D_MODEL), lambda i: (i, 0)),
                pl.BlockSpec((1, D_MODEL), lambda i: (0, 0)),
                pl.BlockSpec((COND_ROWS, D_MODEL), lambda i: (0, sh_chunk)),
                pl.BlockSpec((COND_ROWS, D_MODEL), lambda i: (0, sc_chunk))]
    args = [x, norm_w.reshape(1, D_MODEL), mod, mod]
    if with_router:
        in_specs += [pl.BlockSpec((D_MODEL, N_EXPERTS), lambda i: (0, 0)),
                     pl.BlockSpec((1, N_EXPERTS), lambda i: (0, 0))]
        args += [router_w, router_b.reshape(1, N_EXPERTS)]
        out_specs = [pl.BlockSpec((tm, HALF_D), lambda i: (i, 0)),
                     pl.BlockSpec((tm, TOP_K), lambda i: (i, 0)),
                     pl.BlockSpec((tm, TOP_K), lambda i: (i, 0))]
        out_shape = [jax.ShapeDtypeStruct((N_TOK, HALF_D), jnp.int32),
                     jax.ShapeDtypeStruct((N_TOK, TOP_K), jnp.int32),
                     jax.ShapeDtypeStruct((N_TOK, TOP_K), F32)]
    else:
        out_specs = [pl.BlockSpec((tm, D_MODEL), lambda i: (i, 0))]
        out_shape = [jax.ShapeDtypeStruct((N_TOK, D_MODEL), BF16)]
    out = pl.pallas_call(
        functools.partial(_norm_mod_kernel, tm=tm, with_router=with_router),
        grid=(N_TOK // tm,), in_specs=in_specs, out_specs=out_specs, out_shape=out_shape,
        compiler_params=_cparams(1), name=name)(*args)
    return out if with_router else out[0]


def _rms_kernel(x_ref, w_ref, o_ref):
    o_ref[...] = _rms(x_ref[...], w_ref[...]).astype(o_ref.dtype)


def _rmsnorm_cols(x, w, col_block, width, out_dtype, *, name):
    tm = 256
    m = x.shape[0]
    return pl.pallas_call(
        _rms_kernel, grid=(m // tm,),
        in_specs=[pl.BlockSpec((tm, width), lambda i: (i, col_block)),
                  pl.BlockSpec((1, width), lambda i: (0, 0))],
        out_specs=pl.BlockSpec((tm, width), lambda i: (i, 0)),
        out_shape=jax.ShapeDtypeStruct((m, width), out_dtype),
        compiler_params=_cparams(1), name=name)(x, w.reshape(1, width))


def _rope_tables():
    t = jnp.arange(DEC_SEQ, dtype=jnp.int32)
    row = (t // GRID_W).astype(F32)[:, None]
    col = (t % GRID_W).astype(F32)[:, None]
    half = QK_ROPE // 2
    inv = ROPE_BASE ** (-jnp.arange(0, half, 2, dtype=F32) / half)[None, :]
    ar, ac = row * inv, col * inv
    cos64 = jnp.concatenate([jnp.cos(ar), jnp.cos(ar), jnp.cos(ac), jnp.cos(ac)], axis=-1)
    sin64 = jnp.concatenate([-jnp.sin(ar), jnp.sin(ar), -jnp.sin(ac), jnp.sin(ac)], axis=-1)
    return jnp.tile(cos64, (1, 2)), jnp.tile(sin64, (1, 2))


def _swap16(x):
    lane = lax.broadcasted_iota(jnp.int32, x.shape, x.ndim - 1)
    return jnp.where((lane & 16) == 0,
                     pltpu.roll(x, x.shape[-1] - 16, x.ndim - 1),
                     pltpu.roll(x, 16, x.ndim - 1))


def _rope_kernel(x_ref, cos_ref, sin_ref, o_ref, *, n_chunks):
    cos = cos_ref[...]
    sin = sin_ref[...]
    for c in range(n_chunks):
        x = x_ref[:, c * LANE:(c + 1) * LANE]
        o_ref[:, c * LANE:(c + 1) * LANE] = (x * cos + _swap16(x) * sin).astype(o_ref.dtype)


def _rope_latent(x, cos, sin, col0, width, *, name):
    tm, tc = 256, 2048
    assert width % tc == 0 and col0 % tc == 0
    r0, c0 = N_CTX // tm, col0 // tc
    per_seq = DEC_SEQ // tm
    return pl.pallas_call(
        functools.partial(_rope_kernel, n_chunks=tc // LANE),
        grid=(N_LAT // tm, width // tc),
        in_specs=[pl.BlockSpec((tm, tc), lambda i, j: (r0 + i, c0 + j)),
                  pl.BlockSpec((tm, LANE), lambda i, j: (i % per_seq, 0)),
                  pl.BlockSpec((tm, LANE), lambda i, j: (i % per_seq, 0))],
        out_specs=pl.BlockSpec((tm, tc), lambda i, j: (i, j)),
        out_shape=jax.ShapeDtypeStruct((N_LAT, width), BF16),
        compiler_params=_cparams(2), name=name)(x, cos, sin)


def _rope_kpe_kernel(x_ref, cos_ref, sin_ref, o_ref):
    x = x_ref[...]
    o_ref[...] = x * cos_ref[...] + _swap16(x) * sin_ref[...]


def _rope_kpe(kpe_pad, cos, sin, *, name):
    tm = 256
    per_seq = DEC_SEQ // tm
    return pl.pallas_call(
        _rope_kpe_kernel, grid=(N_LAT // tm,),
        in_specs=[pl.BlockSpec((tm, LANE), lambda i: (i, 0)),
                  pl.BlockSpec((tm, LANE), lambda i: (i % per_seq, 0)),
                  pl.BlockSpec((tm, LANE), lambda i: (i % per_seq, 0))],
        out_specs=pl.BlockSpec((tm, LANE), lambda i: (i, 0)),
        out_shape=jax.ShapeDtypeStruct((N_LAT, LANE), F32),
        compiler_params=_cparams(1), name=name)(kpe_pad, cos, sin)


def _softmax_pv(s_parts, v_parts):
    m = s_parts[0].max(axis=-1, keepdims=True)
    for s in s_parts[1:]:
        m = jnp.maximum(m, s.max(axis=-1, keepdims=True))
    acc, l = None, None
    for s, v in zip(s_parts, v_parts):
        p = jnp.exp(s - m)
        ps = p.sum(axis=-1, keepdims=True)
        pv = jnp.dot(p.astype(BF16), v, preferred_element_type=F32)
        acc = pv if acc is None else acc + pv
        l = ps if l is None else l + ps
    return acc / l


def _diff_lambda(lam_ref, lambda_init):
    lv = lam_ref[...]
    s1 = jnp.sum(lv[0:1] * lv[1:2], axis=-1, keepdims=True)
    s2 = jnp.sum(lv[2:3] * lv[3:4], axis=-1, keepdims=True)
    return jnp.exp(s1) - jnp.exp(s2) + lambda_init


def _diff_finish(o1, o2, lam, subw, lambda_init):
    o = o1 - lam * o2
    return _rms(o, subw) * (1.0 - lambda_init)


def _diff_ctx_kernel(lam_ref, q_ref, k_ref, v_ref, subw_ref, o_ref, ks_ref, vs_ref, *, lambda_init):
    dk = A_QK_DIM
    k = k_ref[...]
    v = v_ref[...]
    ks_ref[...] = k
    vs_ref[...] = v
    q = (q_ref[...] * (dk ** -0.5)).astype(BF16)
    kb = k.astype(BF16)
    vb = v.astype(BF16)
    outs = []
    for lo in (0, dk):
        s = lax.dot_general(q[:, lo:lo + dk], kb[:, lo:lo + dk], _NT_DIMS, preferred_element_type=F32)
        outs.append(_softmax_pv([s], [vb]))
    lam = _diff_lambda(lam_ref, lambda_init)
    o_ref[...] = _diff_finish(outs[0], outs[1], lam, subw_ref[...], lambda_init).astype(BF16)


def _diff_ctx(proj, lamv, subw, lambda_init):
    kc, vc = A_WIDTH // LANE, 2 * A_WIDTH // LANE
    blk = lambda off: pl.BlockSpec((SEQ, LANE), lambda b, h: (b, off + h))
    st = pl.BlockSpec((None, None, SEQ, LANE), lambda b, h: (b, h, 0, 0))
    return pl.pallas_call(
        functools.partial(_diff_ctx_kernel, lambda_init=lambda_init),
        grid=(BATCH, A_HEADS),
        in_specs=[pl.BlockSpec((4, A_QK_DIM), lambda b, h: (0, 0)),
                  blk(0), blk(kc), blk(vc),
                  pl.BlockSpec((1, A_V_DIM), lambda b, h: (0, 0))],
        out_specs=[pl.BlockSpec((SEQ, LANE), lambda b, h: (b, h)), st, st],
        out_shape=[jax.ShapeDtypeStruct((N_CTX, A_WIDTH), BF16),
                   jax.ShapeDtypeStruct((BATCH, A_HEADS, SEQ, 2 * A_QK_DIM), F32),
                   jax.ShapeDtypeStruct((BATCH, A_HEADS, SEQ, A_V_DIM), F32)],
        compiler_params=_cparams(2), name="diff_attn_ctx")(lamv, proj, proj, proj, subw)


def _diff_lat_kernel(lam_ref, q_ref, k_ref, v_ref, ck_ref, cv_ref, subw_ref, o_ref, *, lambda_init):
    dk = A_QK_DIM
    q = q_ref[...] * (dk ** -0.5)
    kb = k_ref[...]
    ckb = ck_ref[...].astype(BF16)
    vb = v_ref[...].astype(BF16)
    cvb = cv_ref[...].astype(BF16)
    outs = []
    for lo in (0, dk):
        qh = q[:, lo:lo + dk]
        sc = lax.dot_general(qh, ckb[:, lo:lo + dk], _NT_DIMS, preferred_element_type=F32)
        sn = lax.dot_general(qh, kb[:, lo:lo + dk], _NT_DIMS, preferred_element_type=F32)
        outs.append(_softmax_pv([sc, sn], [cvb, vb]))
    lam = _diff_lambda(lam_ref, lambda_init)
    o_ref[...] = _diff_finish(outs[0], outs[1], lam, subw_ref[...], lambda_init).astype(BF16)


def _diff_lat(qk_rot, proj, cache_k, cache_v, lamv, subw, lambda_init):
    tq = 256
    nq = DEC_SEQ // tq
    kc, vc = A_WIDTH // LANE, 2 * A_WIDTH // LANE
    v_row0 = N_CTX // DEC_SEQ
    cache = pl.BlockSpec((None, None, PAST_LEN, LANE), lambda b, h, i: (b, h, 0, 0))
    return pl.pallas_call(
        functools.partial(_diff_lat_kernel, lambda_init=lambda_init),
        grid=(DEC_BATCH, A_HEADS, nq),
        in_specs=[pl.BlockSpec((4, A_QK_DIM), lambda b, h, i: (0, 0)),
                  pl.BlockSpec((tq, LANE), lambda b, h, i: (b * nq + i, h)),
                  pl.BlockSpec((DEC_SEQ, LANE), lambda b, h, i: (b, kc + h)),
                  pl.BlockSpec((DEC_SEQ, LANE), lambda b, h, i: (v_row0 + b, vc + h)),
                  cache, cache,
                  pl.BlockSpec((1, A_V_DIM), lambda b, h, i: (0, 0))],
        out_specs=pl.BlockSpec((tq, LANE), lambda b, h, i: (b * nq + i, h)),
        out_shape=jax.ShapeDtypeStruct((N_LAT, A_WIDTH), BF16),
        compiler_params=_cparams(3), name="diff_attn_lat")(
            lamv, qk_rot, qk_rot, proj, cache_k, cache_v, subw)


def _mla_kernel(qn_ref, qp_ref, kv_ref, kpe_ref, o_ref):
    scale = (QK_NOPE + QK_ROPE) ** -0.5
    qn = (qn_ref[...] * scale).astype(BF16)
    qp = (qp_ref[...].astype(F32) * scale).astype(BF16)
    kv = kv_ref[...]
    kpe = kpe_ref[...].astype(BF16)
    for hh in range(2):
        kn = kv[:, hh * 2 * LANE:hh * 2 * LANE + LANE]
        v = kv[:, hh * 2 * LANE + LANE:(hh + 1) * 2 * LANE]
        s = lax.dot_general(qn[:, hh * LANE:(hh + 1) * LANE], kn, _NT_DIMS, preferred_element_type=F32)
        s = s + lax.dot_general(qp[:, hh * QK_ROPE:(hh + 1) * QK_ROPE], kpe, _NT_DIMS,
                                preferred_element_type=F32)
        o_ref[:, hh * LANE:(hh + 1) * LANE] = _softmax_pv([s], [v]).astype(BF16)


def _mla_ctx(q_all, kv_ctx, kpe_all):
    hp = C_HEADS // 2
    pe0 = C_HEADS * QK_NOPE // LANE
    return pl.pallas_call(
        _mla_kernel, grid=(BATCH, hp),
        in_specs=[pl.BlockSpec((SEQ, 2 * LANE), lambda b, h: (b, h)),
                  pl.BlockSpec((SEQ, LANE), lambda b, h: (b, pe0 + h)),
                  pl.BlockSpec((SEQ, 4 * LANE), lambda b, h: (b, h)),
                  pl.BlockSpec((SEQ, QK_ROPE), lambda b, h: (b, 0))],
        out_specs=pl.BlockSpec((SEQ, 2 * LANE), lambda b, h: (b, h)),
        out_shape=jax.ShapeDtypeStruct((N_CTX, C_HEADS * V_DIM), BF16),
        compiler_params=_cparams(2), name="mla_ctx")(q_all, q_all, kv_ctx, kpe_all)


def _mla_lat(q_all, qpe_rot, kv_lat, kpe_lat):
    tq = 256
    nq = DEC_SEQ // tq
    hp = C_HEADS // 2
    lk = PAST_LEN + DEC_SEQ
    r0 = N_CTX // tq
    return pl.pallas_call(
        _mla_kernel, grid=(DEC_BATCH, hp, nq),
        in_specs=[pl.BlockSpec((tq, 2 * LANE), lambda b, h, i: (r0 + b * nq + i, h)),
                  pl.BlockSpec((tq, LANE), lambda b, h, i: (b * nq + i, h)),
                  pl.BlockSpec((lk, 4 * LANE), lambda b, h, i: (b, h)),
                  pl.BlockSpec((None, lk, QK_ROPE), lambda b, h, i: (b, 0, 0))],
        out_specs=pl.BlockSpec((tq, 2 * LANE), lambda b, h, i: (b * nq + i, h)),
        out_shape=jax.ShapeDtypeStruct((N_LAT, C_HEADS * V_DIM), BF16),
        compiler_params=_cparams(3), name="mla_lat")(q_all, qpe_rot, kv_lat, kpe_lat)


def _conv_kernel(gb_ref, gc_ref, x_ref, cw_ref, cb_ref, z_ref, *, seq):
    u = gc_ref[...] * x_ref[...]
    row = lax.broadcasted_iota(jnp.int32, u.shape, 0)
    prev = jnp.where(row == 0, 0.0, pltpu.roll(u, 1, 0))
    nxt = jnp.where(row == seq - 1, 0.0, pltpu.roll(u, seq - 1, 0))
    cw = cw_ref[...]
    y = prev * cw[0:1] + u * cw[1:2] + nxt * cw[2:3] + cb_ref[...]
    z_ref[...] = (gb_ref[...] * y).astype(BF16)


def _short_conv(proj, conv_w, conv_b, row0, n_rows, seq, *, name):
    tc = 512
    base = 3 * A_WIDTH // tc
    nb = B_WIDTH // tc
    r0 = row0 // seq
    blk = lambda off: pl.BlockSpec((seq, tc), lambda i, j: (r0 + i, base + off * nb + j))
    return pl.pallas_call(
        functools.partial(_conv_kernel, seq=seq),
        grid=(n_rows // seq, nb),
        in_specs=[blk(0), blk(1), blk(2),
                  pl.BlockSpec((3, tc), lambda i, j: (0, j)),
                  pl.BlockSpec((1, tc), lambda i, j: (0, j))],
        out_specs=pl.BlockSpec((seq, tc), lambda i, j: (i, j)),
        out_shape=jax.ShapeDtypeStruct((n_rows, B_WIDTH), BF16),
        compiler_params=_cparams(2), name=name)(proj, proj, proj, conv_w, conv_b.reshape(1, B_WIDTH))


def _moe_kernel(ce_ref, crow_ref, cnt_ref, cidx_ref, xs_hbm, wg_ref, wl_ref, bg_ref, bl_ref, wd_ref, bd_ref,
                y_ref, xbuf, xb, act, wa_s, wb_s, sem, *, nj, ta):
    c = pl.program_id(0)
    s = pl.program_id(1)
    n_tiles = cnt_ref[c]
    valid = n_tiles > 0

    def x_copy(ci):
        row0 = pl.multiple_of(crow_ref[ci], TM_MOE)
        return pltpu.make_async_copy(xs_hbm.at[pl.ds(row0, R_CAP)], xbuf, sem)

    @pl.when((c == 0) & (s == 0))
    def _():
        x_copy(0).start()

    @pl.when(valid & (s == 0))
    def _():
        x_copy(c).wait()
        for r in range(TILES_PER_CHUNK):
            rows = slice(r * TM_MOE, (r + 1) * TM_MOE)
            lo, hi = _unpack_bf16_pair(xbuf[rows, :])
            xb[rows, :HALF_D] = lo.astype(BF16)
            xb[rows, HALF_D:] = hi.astype(BF16)
        nxt = jnp.minimum(c + 1, N_CHUNKS - 1)

        @pl.when((c + 1 < N_CHUNKS) & (cnt_ref[nxt] > 0))
        def _():
            x_copy(nxt).start()

    @pl.when(valid & (s < nj))
    def _():
        wa_s[...] = wg_ref[...].astype(BF16)
        wb_s[...] = wl_ref[...].astype(BF16)
        sj = jnp.minimum(s, nj - 1)
        for r in range(TILES_PER_CHUNK):
            rows = slice(r * TM_MOE, (r + 1) * TM_MOE)

            @pl.when(r < n_tiles)
            def _():
                x = xb[rows, :]
                g = jnp.dot(x, wa_s[...], preferred_element_type=F32) + bg_ref[...]
                l = jnp.dot(x, wb_s[...], preferred_element_type=F32) + bl_ref[...]
                glu = jnp.minimum(g, SWIGLU_LIMIT)
                lin = jnp.clip(l, -SWIGLU_LIMIT, SWIGLU_LIMIT)
                act[sj, rows, :] = (glu * jax.nn.sigmoid(SWIGLU_ALPHA * glu) * (lin + 1.0)).astype(BF16)

    @pl.when(valid & (s >= nj))
    def _():
        wa_s[:D_FF, :] = wd_ref[:, :ta].astype(BF16)
        wa_s[D_FF:, :] = wd_ref[:, ta:].astype(BF16)
        for r in range(TILES_PER_CHUNK):
            rows = slice(r * TM_MOE, (r + 1) * TM_MOE)

            @pl.when(r < n_tiles)
            def _():
                lo, hi = bd_ref[:, :ta], bd_ref[:, ta:]
                for q in range(nj):
                    a = act[q, rows, :]
                    lo = lo + jnp.dot(a, wa_s[q * ta:(q + 1) * ta, :], preferred_element_type=F32)
                    hi = hi + jnp.dot(a, wa_s[D_FF + q * ta:D_FF + (q + 1) * ta, :], preferred_element_type=F32)
                y_ref[rows, :] = _pack_bf16_pair(lo, hi)

            @pl.when(r >= n_tiles)
            def _():
                y_ref[rows, :] = jnp.zeros((TM_MOE, ta), jnp.int32)


def _moe_experts(xs, chunk_e, chunk_row, chunk_tiles, chunk_idx, w_gu, b_gu, w_down, b_down):
    ta = 256
    nj = D_FF // ta
    nd = D_MODEL // (2 * ta)

    def gu_j(c, s, cnt):
        return jnp.where(cnt[c] > 0, jnp.minimum(s, nj - 1), nj - 1)

    def dn_j(c, s, cnt):
        return jnp.where(cnt[c] > 0, jnp.maximum(s - nj, 0), nd - 1)

    return pl.pallas_call(
        functools.partial(_moe_kernel, nj=nj, ta=ta),
        grid_spec=pltpu.PrefetchScalarGridSpec(
            num_scalar_prefetch=4, grid=(N_CHUNKS, nj + nd),
            in_specs=[
                pl.BlockSpec(memory_space=pl.ANY),
                pl.BlockSpec((None, D_MODEL, ta), lambda c, s, ce, cr, cnt, ci: (ce[c], 0, gu_j(c, s, cnt))),
                pl.BlockSpec((None, D_MODEL, ta), lambda c, s, ce, cr, cnt, ci: (ce[c], 0, nj + gu_j(c, s, cnt))),
                pl.BlockSpec((None, 1, ta), lambda c, s, ce, cr, cnt, ci: (ce[c], 0, gu_j(c, s, cnt))),
                pl.BlockSpec((None, 1, ta), lambda c, s, ce, cr, cnt, ci: (ce[c], 0, nj + gu_j(c, s, cnt))),
                pl.BlockSpec((None, D_FF, 2 * ta), lambda c, s, ce, cr, cnt, ci: (ce[c], 0, dn_j(c, s, cnt))),
                pl.BlockSpec((None, 1, 2 * ta), lambda c, s, ce, cr, cnt, ci: (ce[c], 0, dn_j(c, s, cnt))),
            ],
            out_specs=pl.BlockSpec((R_CAP, ta), lambda c, s, ce, cr, cnt, ci: (ci[c], dn_j(c, s, cnt))),
            scratch_shapes=[pltpu.VMEM((R_CAP, HALF_D), jnp.int32),
                            pltpu.VMEM((R_CAP, D_MODEL), BF16),
                            pltpu.VMEM((nj, R_CAP, ta), BF16),
                            pltpu.VMEM((D_MODEL, ta), BF16),
                            pltpu.VMEM((D_MODEL, ta), BF16),
                            pltpu.SemaphoreType.DMA(())]),
        out_shape=jax.ShapeDtypeStruct((PY_MOE, HALF_D), jnp.int32),
        compiler_params=pltpu.CompilerParams(dimension_semantics=("arbitrary", "arbitrary"),
                                             vmem_limit_bytes=MOE_VMEM_LIMIT),
        name="moe_experts")(
            chunk_e, chunk_row, chunk_tiles, chunk_idx, xs, w_gu, w_gu,
            b_gu.reshape(N_EXPERTS, 1, 2 * D_FF), b_gu.reshape(N_EXPERTS, 1, 2 * D_FF),
            w_down, b_down.reshape(N_EXPERTS, 1, D_MODEL))


def _moe_routing(top_i):
    e_flat = top_i.reshape(-1)
    experts = jnp.arange(N_EXPERTS, dtype=jnp.int32)
    onehot = (e_flat[:, None] == experts[None, :]).astype(jnp.int32)
    counts = onehot.sum(axis=0)
    rank = jnp.sum((jnp.cumsum(onehot, axis=0) - onehot) * onehot, axis=1)
    padded = ((counts + TM_MOE - 1) // TM_MOE) * TM_MOE
    starts = jnp.cumsum(padded) - padded
    n_ch = (counts + R_CAP - 1) // R_CAP
    ch_end = jnp.cumsum(n_ch)
    ch_base = ch_end - n_ch
    n_chunks = ch_end[-1]
    pos_x = starts[e_flat] + rank
    pos_y = (ch_base[e_flat] + rank // R_CAP) * R_CAP + rank % R_CAP
    row_token = jnp.zeros((PX_MOE,), jnp.int32).at[pos_x].set(
        jnp.arange(N_TOK * TOP_K, dtype=jnp.int32) // TOP_K)
    cid = jnp.arange(N_CHUNKS, dtype=jnp.int32)
    cvalid = cid < n_chunks
    cidx = jnp.minimum(cid, n_chunks - 1)
    chunk_e = jnp.sum((ch_end[None, :] <= cidx[:, None]).astype(jnp.int32), axis=1)
    chunk_e = jnp.minimum(chunk_e, N_EXPERTS - 1)
    k = cidx - ch_base[chunk_e]
    chunk_row = starts[chunk_e] + k * R_CAP
    rows_left = jnp.clip(counts[chunk_e] - k * R_CAP, 0, R_CAP)
    chunk_tiles = jnp.where(cvalid, (rows_left + TM_MOE - 1) // TM_MOE, 0).astype(jnp.int32)
    return pos_y.reshape(N_TOK, TOP_K), row_token, chunk_e, chunk_row.astype(jnp.int32), chunk_tiles, cidx


def _moe_combine_kernel(x_ref, y0_ref, y1_ref, y2_ref, y3_ref, gt_ref, g_ref, o_ref, *, tm):
    g = _group_of_row(pl.program_id(0) * tm)
    half = LANE * 2
    gates = gt_ref[...]
    gk = [jnp.broadcast_to(gates[:, k:k + 1], (tm, half)) for k in range(TOP_K)]
    for blk in range(D_MODEL // (2 * half)):
        lo_acc, hi_acc = None, None
        for k, y_ref in enumerate((y0_ref, y1_ref, y2_ref, y3_ref)):
            lo, hi = _unpack_bf16_pair(y_ref[:, blk * half:(blk + 1) * half])
            lo_acc = gk[k] * lo if lo_acc is None else lo_acc + gk[k] * lo
            hi_acc = gk[k] * hi if hi_acc is None else hi_acc + gk[k] * hi
        for part, acc in enumerate((lo_acc, hi_acc)):
            cols = slice((2 * blk + part) * half, (2 * blk + part + 1) * half)
            o_ref[:, cols] = x_ref[:, cols] + g_ref[pl.ds(g, 1), cols] * acc


def _moe_combine(x, yg, gates, mod, chunk, *, name):
    tm = 256
    nb = N_TOK // tm
    y_spec = lambda k: pl.BlockSpec((tm, HALF_D), lambda i: (k * nb + i, 0))
    return pl.pallas_call(
        functools.partial(_moe_combine_kernel, tm=tm), grid=(nb,),
        in_specs=[pl.BlockSpec((tm, D_MODEL), lambda i: (i, 0)),
                  y_spec(0), y_spec(1), y_spec(2), y_spec(3),
                  pl.BlockSpec((tm, TOP_K), lambda i: (i, 0)),
                  pl.BlockSpec((COND_ROWS, D_MODEL), lambda i: (0, chunk))],
        out_specs=pl.BlockSpec((tm, D_MODEL), lambda i: (i, 0)),
        out_shape=jax.ShapeDtypeStruct((N_TOK, D_MODEL), F32),
        compiler_params=_cparams(1), name=name)(x, yg, yg, yg, yg, gates, mod)


def _moe(x, h_packed, top_i, gates, mod, w_gu, b_gu, w_down, b_down, *, name):
    pos_y, row_token, chunk_e, chunk_row, chunk_tiles, chunk_idx = _moe_routing(top_i)
    xs = jnp.take(h_packed, row_token, axis=0)
    y = _moe_experts(xs, chunk_e, chunk_row, chunk_tiles, chunk_idx, w_gu, b_gu, w_down, b_down)
    yg = jnp.take(y, pos_y.T.reshape(-1), axis=0)
    return _moe_combine(x, yg, gates, mod, 5, name=name)


def _lambda_init(layer):
    return 0.8 - 0.6 * math.exp(-0.3 * layer)


def kernel(x_prompt, x_sample, cache_l0_k, cache_l0_v, cache_l1_ckv, cache_l1_kpe, c, c_ctx, mod0_w, mod0_b, norm0a_w, in0_w, conv0_w, conv0_b, lam0_q1, lam0_k1, lam0_q2, lam0_k2, subln0_w, out0_w, norm0b_w, moe0_router_w, moe0_router_b, moe0_w_gu, moe0_b_gu, moe0_w_down, moe0_b_down, mod1_w, mod1_b, norm1a_w, dqkv1_w, qnorm1_w, kvnorm1_w, uq1_w, ukv1_w, o1_w, norm1b_w, moe1_router_w, moe1_router_b, moe1_w_gu, moe1_b_gu, moe1_w_down, moe1_b_down, final_norm_w):
    x = jnp.concatenate([x_prompt.reshape(N_CTX, D_MODEL), x_sample.reshape(N_LAT, D_MODEL)], axis=0)
    cond = jnp.zeros((COND_ROWS, D_MODEL), F32).at[0].set(c_ctx).at[1:1 + DEC_BATCH].set(c)
    scond = jax.nn.silu(cond).astype(BF16)
    cos, sin = _rope_tables()

    mod = _mm(scond, mod0_w, tm=COND_ROWS, tn=512, out_dtype=F32,
              bias=mod0_b.reshape(1, -1), name="mod0")
    h = _norm_mod(x, norm0a_w, mod, 0, 1, name="norm0a")
    proj = _mm(h, in0_w, tm=1024, tn=512, out_dtype=F32, name="in0_proj")
    qk_rot = _rope_latent(proj, cos, sin, 0, 2 * A_WIDTH, name="rope0")
    lamv = jnp.stack([lam0_q1, lam0_k1, lam0_q2, lam0_k2])
    subw = subln0_w.reshape(1, A_V_DIM)
    li = _lambda_init(0)
    o_ctx, state_l0_k, state_l0_v = _diff_ctx(proj, lamv, subw, li)
    o_lat = _diff_lat(qk_rot, proj, cache_l0_k, cache_l0_v, lamv, subw, li)
    z_ctx = _short_conv(proj, conv0_w, conv0_b, 0, N_CTX, SEQ, name="conv_ctx")
    z_lat = _short_conv(proj, conv0_w, conv0_b, N_CTX, N_LAT, DEC_SEQ, name="conv_lat")
    mix = jnp.concatenate([jnp.concatenate([o_ctx, o_lat], axis=0),
                           jnp.concatenate([z_ctx, z_lat], axis=0)], axis=1)
    x = _mm(mix, out0_w, tm=1024, tn=512, out_dtype=F32, resid=x, gate=mod, gate_chunk=2, name="out0_proj")
    hp, top_i, gates = _norm_mod(x, norm0b_w, mod, 3, 4, moe0_router_w, moe0_router_b, name="norm0b")
    x = _moe(x, hp, top_i, gates, mod, moe0_w_gu, moe0_b_gu, moe0_w_down, moe0_b_down, name="moe0_combine")

    mod = _mm(scond, mod1_w, tm=COND_ROWS, tn=512, out_dtype=F32,
              bias=mod1_b.reshape(1, -1), name="mod1")
    h = _norm_mod(x, norm1a_w, mod, 0, 1, name="norm1a")
    proj = _mm(h, dqkv1_w, tm=1024, tn=256, out_dtype=F32, name="dqkv_proj")
    cq = _rmsnorm_cols(proj, qnorm1_w, 0, Q_LORA, BF16, name="q_norm")
    ckv = _rmsnorm_cols(proj, kvnorm1_w, Q_LORA // KV_LORA, KV_LORA, F32, name="kv_norm")
    kpe = proj[:, Q_LORA + KV_LORA:]
    uq = uq1_w.reshape(Q_LORA, C_HEADS, QK_NOPE + QK_ROPE)
    uq = jnp.concatenate([uq[:, :, :QK_NOPE].reshape(Q_LORA, -1), uq[:, :, QK_NOPE:].reshape(Q_LORA, -1)], axis=1)
    q_all = _mm(cq, uq, tm=1024, tn=512, out_dtype=F32, name="uq_proj")
    qpe_rot = _rope_latent(q_all, cos, sin, C_HEADS * QK_NOPE, C_HEADS * QK_ROPE, name="rope1_q")
    kpe_lat = kpe[N_CTX:]
    kpe_rot = _rope_kpe(jnp.concatenate([kpe_lat, kpe_lat], axis=1), cos, sin, name="rope1_k")[:, :QK_ROPE]
    kpe_lat_all = jnp.concatenate([cache_l1_kpe, kpe_rot.reshape(DEC_BATCH, DEC_SEQ, QK_ROPE)], axis=1)
    ckv_lat_all = jnp.concatenate([cache_l1_ckv, ckv[N_CTX:].reshape(DEC_BATCH, DEC_SEQ, KV_LORA)], axis=1)
    kv_ctx = _mm(ckv[:N_CTX].astype(BF16), ukv1_w, tm=1024, tn=512, out_dtype=BF16, name="ukv_ctx")
    kv_lat = _mm(ckv_lat_all.reshape(-1, KV_LORA).astype(BF16), ukv1_w, tm=1280, tn=512,
                 out_dtype=BF16, name="ukv_lat")
    o_ctx = _mla_ctx(q_all, kv_ctx, kpe)
    o_lat = _mla_lat(q_all, qpe_rot, kv_lat, kpe_lat_all)
    mix = jnp.concatenate([o_ctx, o_lat], axis=0)
    x = _mm(mix, o1_w, tm=1024, tn=512, out_dtype=F32, resid=x, gate=mod, gate_chunk=2, name="o1_proj")
    hp, top_i, gates = _norm_mod(x, norm1b_w, mod, 3, 4, moe1_router_w, moe1_router_b, name="norm1b")
    x = _moe(x, hp, top_i, gates, mod, moe1_w_gu, moe1_b_gu, moe1_w_down, moe1_b_down, name="moe1_combine")

    y = _rmsnorm_cols(x, final_norm_w, 0, D_MODEL, F32, name="final_norm")
    state_l1_ckv = ckv[:N_CTX].reshape(BATCH, SEQ, KV_LORA)
    state_l1_kpe = kpe[:N_CTX].reshape(BATCH, SEQ, QK_ROPE)
    return (y[:N_CTX].reshape(BATCH, SEQ, D_MODEL), y[N_CTX:].reshape(DEC_BATCH, DEC_SEQ, D_MODEL),
            state_l0_k, state_l0_v, state_l1_ckv, state_l1_kpe)
```

```python
import functools
import math

import jax
import jax.numpy as jnp
from jax import lax
from jax.experimental import pallas as pl
from jax.experimental.pallas import tpu as pltpu

F32 = jnp.float32
BF16 = jnp.bfloat16

D_MODEL = 4096
BATCH = 16
SEQ = 256
DEC_BATCH = 2
DEC_SEQ = 1024
PAST_LEN = 256
N_CTX = BATCH * SEQ
N_LAT = DEC_BATCH * DEC_SEQ
N_TOK = N_CTX + N_LAT
GRID_W = 64
NORM_EPS = 1e-6
ROPE_BASE = 10000.0
N_MOD = 6
A_HEADS = 16
A_QK_DIM = 64
A_V_DIM = 128
A_WIDTH = A_HEADS * A_V_DIM
B_WIDTH = D_MODEL - A_WIDTH
C_HEADS = 32
Q_LORA = 1024
KV_LORA = 512
QK_NOPE = 128
QK_ROPE = 64
V_DIM = 128
N_EXPERTS = 32
TOP_K = 4
D_FF = 2048
SWIGLU_ALPHA = 1.702
SWIGLU_LIMIT = 7.0

LANE = 128
COND_ROWS = 16
ATTN_HB = 4
VMEM_LIMIT = 56 * 1024 * 1024
MOE_VMEM_LIMIT = 60 * 1024 * 1024
TM_MOE = 128
R_CAP = 1024
TILES_PER_CHUNK = R_CAP // TM_MOE
MOE_ROW_STEPS = (768, 896, R_CAP)
N_CHUNKS = N_EXPERTS + (N_TOK * TOP_K) // R_CAP
PX_MOE = N_TOK * TOP_K + N_EXPERTS * TM_MOE + R_CAP
PY_MOE = N_CHUNKS * R_CAP
HALF_D = D_MODEL // 2
HI_MASK = -65536

_NT_DIMS = (((1,), (1,)), ((), ()))


def _cparams(n_axes):
    return pltpu.CompilerParams(dimension_semantics=("arbitrary",) * n_axes,
                                vmem_limit_bytes=VMEM_LIMIT)


def _group_of_row(r0):
    return jnp.where(r0 < N_CTX, 0, 1 + (r0 - N_CTX) // DEC_SEQ)


def _mm_kernel(*refs, mode, tm):
    if mode == "plain":
        x_ref, w_ref, o_ref = refs
    elif mode == "bias":
        x_ref, w_ref, b_ref, o_ref = refs
    else:
        x_ref, w_ref, g_ref, r_ref, o_ref = refs
    acc = jnp.dot(x_ref[...], w_ref[...].astype(BF16), preferred_element_type=F32)
    if mode == "bias":
        acc = acc + b_ref[...]
    elif mode == "resid":
        g = _group_of_row(pl.program_id(0) * tm)
        acc = r_ref[...] + g_ref[pl.ds(g, 1), :] * acc
    o_ref[...] = acc.astype(o_ref.dtype)


def _mm(x, w, *, tm, tn, out_dtype, bias=None, resid=None, gate=None, gate_chunk=0, name):
    m, k = x.shape
    n = w.shape[1]
    assert m % tm == 0
    grid = (m // tm, pl.cdiv(n, tn))
    in_specs = [pl.BlockSpec((tm, k), lambda i, j: (i, 0)),
                pl.BlockSpec((k, tn), lambda i, j: (0, j))]
    args = [x, w]
    mode = "plain"
    if bias is not None:
        mode = "bias"
        in_specs.append(pl.BlockSpec((1, tn), lambda i, j: (0, j)))
        args.append(bias)
    elif resid is not None:
        mode = "resid"
        off = gate_chunk * (D_MODEL // tn)
        in_specs.append(pl.BlockSpec((gate.shape[0], tn), lambda i, j: (0, off + j)))
        in_specs.append(pl.BlockSpec((tm, tn), lambda i, j: (i, j)))
        args += [gate, resid]
    return pl.pallas_call(
        functools.partial(_mm_kernel, mode=mode, tm=tm),
        grid=grid, in_specs=in_specs,
        out_specs=pl.BlockSpec((tm, tn), lambda i, j: (i, j)),
        out_shape=jax.ShapeDtypeStruct((m, n), out_dtype),
        compiler_params=_cparams(2), name=name)(*args)


def _rms(x, w):
    return x * lax.rsqrt(jnp.mean(x * x, axis=-1, keepdims=True) + NORM_EPS) * w


def _pack_bf16_pair(lo, hi):
    lo_b = lax.bitcast_convert_type(lo.astype(BF16).astype(F32), jnp.int32)
    hi_b = lax.bitcast_convert_type(hi.astype(BF16).astype(F32), jnp.int32)
    return hi_b | ((lo_b >> 16) & 0xFFFF)


def _unpack_bf16_pair(w):
    return (lax.bitcast_convert_type(w << 16, F32), lax.bitcast_convert_type(w & HI_MASK, F32))


def _top_k_gates(logits):
    lane = lax.broadcasted_iota(jnp.int32, logits.shape, 1)
    vals, idxs = [], []
    for _ in range(TOP_K):
        m = logits.max(axis=-1, keepdims=True)
        idx = jnp.min(jnp.where(logits == m, lane, N_EXPERTS), axis=-1, keepdims=True)
        vals.append(m)
        idxs.append(idx)
        logits = jnp.where(lane == idx, -jnp.inf, logits)
    exps = [jnp.exp(v - vals[0]) for v in vals]
    denom = exps[0]
    for e in exps[1:]:
        denom = denom + e
    return idxs, [e / denom for e in exps]


def _norm_mod_kernel(x_ref, w_ref, sh_ref, sc_ref, *rest, tm, with_router):
    g = _group_of_row(pl.program_id(0) * tm)
    y = _rms(x_ref[...], w_ref[...])
    h = y * (1.0 + sc_ref[pl.ds(g, 1), :]) + sh_ref[pl.ds(g, 1), :]
    if not with_router:
        (h_ref,) = rest
        h_ref[...] = h.astype(BF16)
        return
    rw_ref, rb_ref, hp_ref, ti_ref, gt_ref, rk_ref, cnt_ref, seen = rest
    logits = jnp.dot(h, rw_ref[...], preferred_element_type=F32,
                     precision=lax.Precision.HIGHEST) + rb_ref[...]
    idxs, gates = _top_k_gates(logits)

    @pl.when(pl.program_id(0) == 0)
    def _():
        seen[...] = jnp.zeros_like(seen)

    before = (lax.broadcasted_iota(jnp.int32, (tm, tm), 1)
              < lax.broadcasted_iota(jnp.int32, (tm, tm), 0)).astype(BF16)
    lane = lax.broadcasted_iota(jnp.int32, (tm, N_EXPERTS), 1)
    count = seen[...]
    for k in range(TOP_K):
        onehot = (lane == idxs[k]).astype(F32)
        earlier = jnp.dot(before, onehot.astype(BF16), preferred_element_type=F32) + count
        rk_ref[:, k:k + 1] = jnp.sum(onehot * earlier, axis=-1, keepdims=True).astype(jnp.int32)
        count = count + jnp.sum(onehot, axis=0, keepdims=True)
        ti_ref[:, k:k + 1] = idxs[k]
        gt_ref[:, k:k + 1] = gates[k]
    seen[...] = count
    cnt_ref[...] = count.astype(jnp.int32)
    hp_ref[...] = _pack_bf16_pair(h[:, :HALF_D], h[:, HALF_D:])


def _norm_mod(x, norm_w, mod, sh_chunk, sc_chunk, router_w=None, router_b=None, *, name):
    tm = 256
    with_router = router_w is not None
    in_specs = [pl.BlockSpec((tm, D_MODEL), lambda i: (i, 0)),
                pl.BlockSpec((1, D_MODEL), lambda i: (0, 0)),
                pl.BlockSpec((COND_ROWS, D_MODEL), lambda i: (0, sh_chunk)),
                pl.BlockSpec((COND_ROWS, D_MODEL), lambda i: (0, sc_chunk))]
    args = [x, norm_w.reshape(1, D_MODEL), mod, mod]
    if with_router:
        in_specs += [pl.BlockSpec((D_MODEL, N_EXPERTS), lambda i: (0, 0)),
                     pl.BlockSpec((1, N_EXPERTS), lambda i: (0, 0))]
        args += [router_w, router_b.reshape(1, N_EXPERTS)]
        slot = pl.BlockSpec((tm, TOP_K), lambda i: (i, 0))
        out_specs = [pl.BlockSpec((tm, HALF_D), lambda i: (i, 0)), slot, slot, slot,
                     pl.BlockSpec((1, N_EXPERTS), lambda i: (0, 0))]
        out_shape = [jax.ShapeDtypeStruct((N_TOK, HALF_D), jnp.int32),
                     jax.ShapeDtypeStruct((N_TOK, TOP_K), jnp.int32),
                     jax.ShapeDtypeStruct((N_TOK, TOP_K), F32),
                     jax.ShapeDtypeStruct((N_TOK, TOP_K), jnp.int32),
                     jax.ShapeDtypeStruct((1, N_EXPERTS), jnp.int32)]
        scratch = [pltpu.VMEM((1, N_EXPERTS), F32)]
    else:
        out_specs = [pl.BlockSpec((tm, D_MODEL), lambda i: (i, 0))]
        out_shape = [jax.ShapeDtypeStruct((N_TOK, D_MODEL), BF16)]
        scratch = []
    out = pl.pallas_call(
        functools.partial(_norm_mod_kernel, tm=tm, with_router=with_router),
        grid=(N_TOK // tm,), in_specs=in_specs, out_specs=out_specs, out_shape=out_shape,
        scratch_shapes=scratch, compiler_params=_cparams(1), name=name)(*args)
    return out if with_router else out[0]


def _rms_kernel(x_ref, w_ref, o_ref):
    o_ref[...] = _rms(x_ref[...], w_ref[...]).astype(o_ref.dtype)


def _rmsnorm_cols(x, w, col_block, width, out_dtype, *, name):
    tm = 256
    m = x.shape[0]
    return pl.pallas_call(
        _rms_kernel, grid=(m // tm,),
        in_specs=[pl.BlockSpec((tm, width), lambda i: (i, col_block)),
                  pl.BlockSpec((1, width), lambda i: (0, 0))],
        out_specs=pl.BlockSpec((tm, width), lambda i: (i, 0)),
        out_shape=jax.ShapeDtypeStruct((m, width), out_dtype),
        compiler_params=_cparams(1), name=name)(x, w.reshape(1, width))


def _rope_tables():
    t = jnp.arange(DEC_SEQ, dtype=jnp.int32)
    row = (t // GRID_W).astype(F32)[:, None]
    col = (t % GRID_W).astype(F32)[:, None]
    half = QK_ROPE // 2
    inv = ROPE_BASE ** (-jnp.arange(0, half, 2, dtype=F32) / half)[None, :]
    ar, ac = row * inv, col * inv
    cos64 = jnp.concatenate([jnp.cos(ar), jnp.cos(ar), jnp.cos(ac), jnp.cos(ac)], axis=-1)
    sin64 = jnp.concatenate([-jnp.sin(ar), jnp.sin(ar), -jnp.sin(ac), jnp.sin(ac)], axis=-1)
    return jnp.tile(cos64, (1, 2)), jnp.tile(sin64, (1, 2))


def _swap16(x):
    lane = lax.broadcasted_iota(jnp.int32, x.shape, x.ndim - 1)
    return jnp.where((lane & 16) == 0,
                     pltpu.roll(x, x.shape[-1] - 16, x.ndim - 1),
                     pltpu.roll(x, 16, x.ndim - 1))


def _rope_kernel(x_ref, cos_ref, sin_ref, o_ref, *, n_chunks):
    cos = cos_ref[...]
    sin = sin_ref[...]
    for c in range(n_chunks):
        x = x_ref[:, c * LANE:(c + 1) * LANE]
        o_ref[:, c * LANE:(c + 1) * LANE] = (x * cos + _swap16(x) * sin).astype(o_ref.dtype)


def _rope_latent(x, cos, sin, col0, width, *, name):
    tm, tc = 256, 2048
    assert width % tc == 0 and col0 % tc == 0
    r0, c0 = N_CTX // tm, col0 // tc
    per_seq = DEC_SEQ // tm
    return pl.pallas_call(
        functools.partial(_rope_kernel, n_chunks=tc // LANE),
        grid=(N_LAT // tm, width // tc),
        in_specs=[pl.BlockSpec((tm, tc), lambda i, j: (r0 + i, c0 + j)),
                  pl.BlockSpec((tm, LANE), lambda i, j: (i % per_seq, 0)),
                  pl.BlockSpec((tm, LANE), lambda i, j: (i % per_seq, 0))],
        out_specs=pl.BlockSpec((tm, tc), lambda i, j: (i, j)),
        out_shape=jax.ShapeDtypeStruct((N_LAT, width), BF16),
        compiler_params=_cparams(2), name=name)(x, cos, sin)


def _rope_kpe_kernel(x_ref, cos_ref, sin_ref, o_ref):
    x = x_ref[...]
    o_ref[...] = x * cos_ref[...] + _swap16(x) * sin_ref[...]


def _rope_kpe(kpe_pad, cos, sin, *, name):
    tm = 256
    per_seq = DEC_SEQ // tm
    return pl.pallas_call(
        _rope_kpe_kernel, grid=(N_LAT // tm,),
        in_specs=[pl.BlockSpec((tm, LANE), lambda i: (i, 0)),
                  pl.BlockSpec((tm, LANE), lambda i: (i % per_seq, 0)),
                  pl.BlockSpec((tm, LANE), lambda i: (i % per_seq, 0))],
        out_specs=pl.BlockSpec((tm, LANE), lambda i: (i, 0)),
        out_shape=jax.ShapeDtypeStruct((N_LAT, LANE), F32),
        compiler_params=_cparams(1), name=name)(kpe_pad, cos, sin)


def _softmax_pv(s_parts, v_parts):
    m = s_parts[0].max(axis=-1, keepdims=True)
    for s in s_parts[1:]:
        m = jnp.maximum(m, s.max(axis=-1, keepdims=True))
    acc, l = None, None
    for s, v in zip(s_parts, v_parts):
        p = jnp.exp(s - m)
        ps = p.sum(axis=-1, keepdims=True)
        pv = jnp.dot(p.astype(BF16), v, preferred_element_type=F32)
        acc = pv if acc is None else acc + pv
        l = ps if l is None else l + ps
    return acc / l


def _diff_lambda(lam_ref, lambda_init):
    lv = lam_ref[...]
    s1 = jnp.sum(lv[0:1] * lv[1:2], axis=-1, keepdims=True)
    s2 = jnp.sum(lv[2:3] * lv[3:4], axis=-1, keepdims=True)
    return jnp.exp(s1) - jnp.exp(s2) + lambda_init


def _diff_finish(o1, o2, lam, subw, lambda_init):
    o = o1 - lam * o2
    return _rms(o, subw) * (1.0 - lambda_init)


def _diff_ctx_kernel(lam_ref, q_ref, k_ref, v_ref, subw_ref, o_ref, ks_ref, vs_ref, *, lambda_init):
    dk = A_QK_DIM
    lam = _diff_lambda(lam_ref, lambda_init)
    for h in range(ATTN_HB):
        cols = slice(h * LANE, (h + 1) * LANE)
        k = k_ref[:, cols]
        v = v_ref[:, cols]
        ks_ref[h] = k
        vs_ref[h] = v
        q = (q_ref[:, cols] * (dk ** -0.5)).astype(BF16)
        kb = k.astype(BF16)
        vb = v.astype(BF16)
        outs = []
        for lo in (0, dk):
            s = lax.dot_general(q[:, lo:lo + dk], kb[:, lo:lo + dk], _NT_DIMS, preferred_element_type=F32)
            outs.append(_softmax_pv([s], [vb]))
        o_ref[:, cols] = _diff_finish(outs[0], outs[1], lam, subw_ref[...], lambda_init).astype(BF16)


def _diff_ctx(proj, lamv, subw, lambda_init):
    w = ATTN_HB * LANE
    kc, vc = A_WIDTH // w, 2 * A_WIDTH // w
    blk = lambda off: pl.BlockSpec((SEQ, w), lambda b, h: (b, off + h))
    st = pl.BlockSpec((None, ATTN_HB, SEQ, LANE), lambda b, h: (b, h, 0, 0))
    return pl.pallas_call(
        functools.partial(_diff_ctx_kernel, lambda_init=lambda_init),
        grid=(BATCH, A_HEADS // ATTN_HB),
        in_specs=[pl.BlockSpec((4, A_QK_DIM), lambda b, h: (0, 0)),
                  blk(0), blk(kc), blk(vc),
                  pl.BlockSpec((1, A_V_DIM), lambda b, h: (0, 0))],
        out_specs=[pl.BlockSpec((SEQ, w), lambda b, h: (b, h)), st, st],
        out_shape=[jax.ShapeDtypeStruct((N_CTX, A_WIDTH), BF16),
                   jax.ShapeDtypeStruct((BATCH, A_HEADS, SEQ, 2 * A_QK_DIM), F32),
                   jax.ShapeDtypeStruct((BATCH, A_HEADS, SEQ, A_V_DIM), F32)],
        compiler_params=_cparams(2), name="diff_attn_ctx")(lamv, proj, proj, proj, subw)


def _diff_lat_kernel(lam_ref, q_ref, k_ref, v_ref, ck_ref, cv_ref, subw_ref, o_ref, *, lambda_init):
    dk = A_QK_DIM
    lam = _diff_lambda(lam_ref, lambda_init)
    for h in range(ATTN_HB):
        cols = slice(h * LANE, (h + 1) * LANE)
        q = q_ref[:, cols] * (dk ** -0.5)
        kb = k_ref[:, cols]
        ckb = ck_ref[h].astype(BF16)
        vb = v_ref[:, cols].astype(BF16)
        cvb = cv_ref[h].astype(BF16)
        outs = []
        for lo in (0, dk):
            qh = q[:, lo:lo + dk]
            sc = lax.dot_general(qh, ckb[:, lo:lo + dk], _NT_DIMS, preferred_element_type=F32)
            sn = lax.dot_general(qh, kb[:, lo:lo + dk], _NT_DIMS, preferred_element_type=F32)
            outs.append(_softmax_pv([sc, sn], [cvb, vb]))
        o_ref[:, cols] = _diff_finish(outs[0], outs[1], lam, subw_ref[...], lambda_init).astype(BF16)


def _diff_lat(qk_rot, proj, cache_k, cache_v, lamv, subw, lambda_init):
    tq = 256
    nq = DEC_SEQ // tq
    w = ATTN_HB * LANE
    kc, vc = A_WIDTH // w, 2 * A_WIDTH // w
    v_row0 = N_CTX // DEC_SEQ
    cache = pl.BlockSpec((None, ATTN_HB, PAST_LEN, LANE), lambda b, h, i: (b, h, 0, 0))
    return pl.pallas_call(
        functools.partial(_diff_lat_kernel, lambda_init=lambda_init),
        grid=(DEC_BATCH, A_HEADS // ATTN_HB, nq),
        in_specs=[pl.BlockSpec((4, A_QK_DIM), lambda b, h, i: (0, 0)),
                  pl.BlockSpec((tq, w), lambda b, h, i: (b * nq + i, h)),
                  pl.BlockSpec((DEC_SEQ, w), lambda b, h, i: (b, kc + h)),
                  pl.BlockSpec((DEC_SEQ, w), lambda b, h, i: (v_row0 + b, vc + h)),
                  cache, cache,
                  pl.BlockSpec((1, A_V_DIM), lambda b, h, i: (0, 0))],
        out_specs=pl.BlockSpec((tq, w), lambda b, h, i: (b * nq + i, h)),
        out_shape=jax.ShapeDtypeStruct((N_LAT, A_WIDTH), BF16),
        compiler_params=_cparams(3), name="diff_attn_lat")(
            lamv, qk_rot, qk_rot, proj, cache_k, cache_v, subw)


def _mla_kernel(qn_ref, qp_ref, kv_ref, kpe_ref, o_ref):
    scale = (QK_NOPE + QK_ROPE) ** -0.5
    kpe = kpe_ref[...].astype(BF16)
    for h in range(ATTN_HB):
        qn = (qn_ref[:, h * LANE:(h + 1) * LANE] * scale).astype(BF16)
        qp = (qp_ref[:, h * QK_ROPE:(h + 1) * QK_ROPE].astype(F32) * scale).astype(BF16)
        kn = kv_ref[:, 2 * h * LANE:(2 * h + 1) * LANE]
        v = kv_ref[:, (2 * h + 1) * LANE:(2 * h + 2) * LANE]
        s = lax.dot_general(qn, kn, _NT_DIMS, preferred_element_type=F32)
        s = s + lax.dot_general(qp, kpe, _NT_DIMS, preferred_element_type=F32)
        o_ref[:, h * LANE:(h + 1) * LANE] = _softmax_pv([s], [v]).astype(BF16)


def _mla_ctx(q_all, kv_ctx, kpe_all):
    hg = C_HEADS // ATTN_HB
    pe0 = C_HEADS * QK_NOPE // (ATTN_HB * QK_ROPE)
    return pl.pallas_call(
        _mla_kernel, grid=(BATCH, hg),
        in_specs=[pl.BlockSpec((SEQ, ATTN_HB * LANE), lambda b, h: (b, h)),
                  pl.BlockSpec((SEQ, ATTN_HB * QK_ROPE), lambda b, h: (b, pe0 + h)),
                  pl.BlockSpec((SEQ, 2 * ATTN_HB * LANE), lambda b, h: (b, h)),
                  pl.BlockSpec((SEQ, QK_ROPE), lambda b, h: (b, 0))],
        out_specs=pl.BlockSpec((SEQ, ATTN_HB * LANE), lambda b, h: (b, h)),
        out_shape=jax.ShapeDtypeStruct((N_CTX, C_HEADS * V_DIM), BF16),
        compiler_params=_cparams(2), name="mla_ctx")(q_all, q_all, kv_ctx, kpe_all)


def _mla_lat(q_all, qpe_rot, kv_lat, kpe_lat):
    tq = 256
    nq = DEC_SEQ // tq
    hg = C_HEADS // ATTN_HB
    lk = PAST_LEN + DEC_SEQ
    r0 = N_CTX // tq
    return pl.pallas_call(
        _mla_kernel, grid=(DEC_BATCH, hg, nq),
        in_specs=[pl.BlockSpec((tq, ATTN_HB * LANE), lambda b, h, i: (r0 + b * nq + i, h)),
                  pl.BlockSpec((tq, ATTN_HB * QK_ROPE), lambda b, h, i: (b * nq + i, h)),
                  pl.BlockSpec((lk, 2 * ATTN_HB * LANE), lambda b, h, i: (b, h)),
                  pl.BlockSpec((None, lk, QK_ROPE), lambda b, h, i: (b, 0, 0))],
        out_specs=pl.BlockSpec((tq, ATTN_HB * LANE), lambda b, h, i: (b * nq + i, h)),
        out_shape=jax.ShapeDtypeStruct((N_LAT, C_HEADS * V_DIM), BF16),
        compiler_params=_cparams(3), name="mla_lat")(q_all, qpe_rot, kv_lat, kpe_lat)


def _conv_kernel(gb_ref, gc_ref, x_ref, cw_ref, cb_ref, z_ref, *, seq):
    u = gc_ref[...] * x_ref[...]
    row = lax.broadcasted_iota(jnp.int32, u.shape, 0)
    prev = jnp.where(row == 0, 0.0, pltpu.roll(u, 1, 0))
    nxt = jnp.where(row == seq - 1, 0.0, pltpu.roll(u, seq - 1, 0))
    cw = cw_ref[...]
    y = prev * cw[0:1] + u * cw[1:2] + nxt * cw[2:3] + cb_ref[...]
    z_ref[...] = (gb_ref[...] * y).astype(BF16)


def _short_conv(proj, conv_w, conv_b, row0, n_rows, seq, *, name):
    tc = 512
    base = 3 * A_WIDTH // tc
    nb = B_WIDTH // tc
    r0 = row0 // seq
    blk = lambda off: pl.BlockSpec((seq, tc), lambda i, j: (r0 + i, base + off * nb + j))
    return pl.pallas_call(
        functools.partial(_conv_kernel, seq=seq),
        grid=(n_rows // seq, nb),
        in_specs=[blk(0), blk(1), blk(2),
                  pl.BlockSpec((3, tc), lambda i, j: (0, j)),
                  pl.BlockSpec((1, tc), lambda i, j: (0, j))],
        out_specs=pl.BlockSpec((seq, tc), lambda i, j: (i, j)),
        out_shape=jax.ShapeDtypeStruct((n_rows, B_WIDTH), BF16),
        compiler_params=_cparams(2), name=name)(proj, proj, proj, conv_w, conv_b.reshape(1, B_WIDTH))


def _moe_kernel(ce_ref, crow_ref, cnt_ref, cidx_ref, xs_hbm, wg_ref, wl_ref, bg_ref, bl_ref, wd_ref, bd_ref,
                y_ref, xbuf, xb, act, sem, *, nj, ta):
    c = pl.program_id(0)
    s = pl.program_id(1)
    n_tiles = cnt_ref[c]
    valid = n_tiles > 0

    def x_copy(ci):
        row0 = pl.multiple_of(crow_ref[ci], TM_MOE)
        return pltpu.make_async_copy(xs_hbm.at[pl.ds(row0, R_CAP)], xbuf, sem)

    @pl.when((c == 0) & (s == 0))
    def _():
        x_copy(0).start()

    @pl.when(valid & (s == 0))
    def _():
        x_copy(c).wait()
        for r in range(TILES_PER_CHUNK):
            rows = slice(r * TM_MOE, (r + 1) * TM_MOE)
            lo, hi = _unpack_bf16_pair(xbuf[rows, :])
            xb[rows, :HALF_D] = lo.astype(BF16)
            xb[rows, HALF_D:] = hi.astype(BF16)
        nxt = jnp.minimum(c + 1, N_CHUNKS - 1)

        @pl.when((c + 1 < N_CHUNKS) & (cnt_ref[nxt] > 0))
        def _():
            x_copy(nxt).start()

    def for_row_count(body):
        lo_tiles = 0
        for m in MOE_ROW_STEPS:
            hi_tiles = m // TM_MOE
            pl.when(valid & (n_tiles > lo_tiles) & (n_tiles <= hi_tiles))(functools.partial(body, m))
            lo_tiles = hi_tiles

    def gate_up(m):
        x = xb[:m, :]
        g = jnp.dot(x, wg_ref[...].astype(BF16), preferred_element_type=F32) + bg_ref[...]
        l = jnp.dot(x, wl_ref[...].astype(BF16), preferred_element_type=F32) + bl_ref[...]
        glu = jnp.minimum(g, SWIGLU_LIMIT)
        lin = jnp.clip(l, -SWIGLU_LIMIT, SWIGLU_LIMIT)
        col0 = pl.multiple_of(jnp.minimum(s, nj - 1) * ta, ta)
        act[:m, pl.ds(col0, ta)] = (glu * jax.nn.sigmoid(SWIGLU_ALPHA * glu) * (lin + 1.0)).astype(BF16)

    def down(m):
        a = act[:m, :]
        lo = jnp.dot(a, wd_ref[:, :ta].astype(BF16), preferred_element_type=F32) + bd_ref[:, :ta]
        hi = jnp.dot(a, wd_ref[:, ta:].astype(BF16), preferred_element_type=F32) + bd_ref[:, ta:]
        y_ref[:m, :] = _pack_bf16_pair(lo, hi)
        if m < R_CAP:
            y_ref[m:, :] = jnp.zeros((R_CAP - m, ta), jnp.int32)

    @pl.when(s < nj)
    def _():
        for_row_count(gate_up)

    @pl.when(s >= nj)
    def _():
        for_row_count(down)


def _moe_experts(xs, chunk_e, chunk_row, chunk_tiles, chunk_idx, w_gu, b_gu, w_down, b_down):
    ta = 256
    nj = D_FF // ta
    nd = D_MODEL // (2 * ta)

    def gu_j(c, s, cnt):
        return jnp.where(cnt[c] > 0, jnp.minimum(s, nj - 1), nj - 1)

    def dn_j(c, s, cnt):
        return jnp.where(cnt[c] > 0, jnp.maximum(s - nj, 0), nd - 1)

    return pl.pallas_call(
        functools.partial(_moe_kernel, nj=nj, ta=ta),
        grid_spec=pltpu.PrefetchScalarGridSpec(
            num_scalar_prefetch=4, grid=(N_CHUNKS, nj + nd),
            in_specs=[
                pl.BlockSpec(memory_space=pl.ANY),
                pl.BlockSpec((None, D_MODEL, ta), lambda c, s, ce, cr, cnt, ci: (ce[c], 0, gu_j(c, s, cnt))),
                pl.BlockSpec((None, D_MODEL, ta), lambda c, s, ce, cr, cnt, ci: (ce[c], 0, nj + gu_j(c, s, cnt))),
                pl.BlockSpec((None, 1, ta), lambda c, s, ce, cr, cnt, ci: (ce[c], 0, gu_j(c, s, cnt))),
                pl.BlockSpec((None, 1, ta), lambda c, s, ce, cr, cnt, ci: (ce[c], 0, nj + gu_j(c, s, cnt))),
                pl.BlockSpec((None, D_FF, 2 * ta), lambda c, s, ce, cr, cnt, ci: (ce[c], 0, dn_j(c, s, cnt))),
                pl.BlockSpec((None, 1, 2 * ta), lambda c, s, ce, cr, cnt, ci: (ce[c], 0, dn_j(c, s, cnt))),
            ],
            out_specs=pl.BlockSpec((R_CAP, ta), lambda c, s, ce, cr, cnt, ci: (ci[c], dn_j(c, s, cnt))),
            scratch_shapes=[pltpu.VMEM((R_CAP, HALF_D), jnp.int32),
                            pltpu.VMEM((R_CAP, D_MODEL), BF16),
                            pltpu.VMEM((R_CAP, D_FF), BF16),
                            pltpu.SemaphoreType.DMA(())]),
        out_shape=jax.ShapeDtypeStruct((PY_MOE, HALF_D), jnp.int32),
        compiler_params=pltpu.CompilerParams(dimension_semantics=("arbitrary", "arbitrary"),
                                             vmem_limit_bytes=MOE_VMEM_LIMIT),
        name="moe_experts")(
            chunk_e, chunk_row, chunk_tiles, chunk_idx, xs, w_gu, w_gu,
            b_gu.reshape(N_EXPERTS, 1, 2 * D_FF), b_gu.reshape(N_EXPERTS, 1, 2 * D_FF),
            w_down, b_down.reshape(N_EXPERTS, 1, D_MODEL))


def _moe_routing(top_i, rank, counts):
    e_flat = top_i.reshape(-1)
    rank = rank.reshape(-1)
    counts = counts.reshape(-1)
    padded = ((counts + TM_MOE - 1) // TM_MOE) * TM_MOE
    starts = jnp.cumsum(padded) - padded
    n_ch = (counts + R_CAP - 1) // R_CAP
    ch_end = jnp.cumsum(n_ch)
    ch_base = ch_end - n_ch
    n_chunks = ch_end[-1]
    pos_x = starts[e_flat] + rank
    pos_y = (ch_base[e_flat] + rank // R_CAP) * R_CAP + rank % R_CAP
    row_token = jnp.zeros((PX_MOE,), jnp.int32).at[pos_x].set(
        jnp.arange(N_TOK * TOP_K, dtype=jnp.int32) // TOP_K)
    cid = jnp.arange(N_CHUNKS, dtype=jnp.int32)
    cvalid = cid < n_chunks
    cidx = jnp.minimum(cid, n_chunks - 1)
    chunk_e = jnp.sum((ch_end[None, :] <= cidx[:, None]).astype(jnp.int32), axis=1)
    chunk_e = jnp.minimum(chunk_e, N_EXPERTS - 1)
    k = cidx - ch_base[chunk_e]
    chunk_row = starts[chunk_e] + k * R_CAP
    rows_left = jnp.clip(counts[chunk_e] - k * R_CAP, 0, R_CAP)
    chunk_tiles = jnp.where(cvalid, (rows_left + TM_MOE - 1) // TM_MOE, 0).astype(jnp.int32)
    return pos_y.reshape(N_TOK, TOP_K), row_token, chunk_e, chunk_row.astype(jnp.int32), chunk_tiles, cidx


def _moe_combine_kernel(x_ref, y0_ref, y1_ref, y2_ref, y3_ref, gt_ref, g_ref, o_ref, *, tm):
    g = _group_of_row(pl.program_id(0) * tm)
    half = LANE * 2
    gates = gt_ref[...]
    gk = [jnp.broadcast_to(gates[:, k:k + 1], (tm, half)) for k in range(TOP_K)]
    for blk in range(D_MODEL // (2 * half)):
        lo_acc, hi_acc = None, None
        for k, y_ref in enumerate((y0_ref, y1_ref, y2_ref, y3_ref)):
            lo, hi = _unpack_bf16_pair(y_ref[:, blk * half:(blk + 1) * half])
            lo_acc = gk[k] * lo if lo_acc is None else lo_acc + gk[k] * lo
            hi_acc = gk[k] * hi if hi_acc is None else hi_acc + gk[k] * hi
        for part, acc in enumerate((lo_acc, hi_acc)):
            cols = slice((2 * blk + part) * half, (2 * blk + part + 1) * half)
            o_ref[:, cols] = x_ref[:, cols] + g_ref[pl.ds(g, 1), cols] * acc


def _moe_combine(x, yg, gates, mod, chunk, *, name):
    tm = 256
    nb = N_TOK // tm
    y_spec = lambda k: pl.BlockSpec((tm, HALF_D), lambda i: (k * nb + i, 0))
    return pl.pallas_call(
        functools.partial(_moe_combine_kernel, tm=tm), grid=(nb,),
        in_specs=[pl.BlockSpec((tm, D_MODEL), lambda i: (i, 0)),
                  y_spec(0), y_spec(1), y_spec(2), y_spec(3),
                  pl.BlockSpec((tm, TOP_K), lambda i: (i, 0)),
                  pl.BlockSpec((COND_ROWS, D_MODEL), lambda i: (0, chunk))],
        out_specs=pl.BlockSpec((tm, D_MODEL), lambda i: (i, 0)),
        out_shape=jax.ShapeDtypeStruct((N_TOK, D_MODEL), F32),
        compiler_params=_cparams(1), name=name)(x, yg, yg, yg, yg, gates, mod)


def _moe(x, routed, mod, w_gu, b_gu, w_down, b_down, *, name):
    h_packed, top_i, gates, rank, counts = routed
    pos_y, row_token, chunk_e, chunk_row, chunk_tiles, chunk_idx = _moe_routing(top_i, rank, counts)
    xs = h_packed.at[row_token].get(mode="promise_in_bounds")
    y = _moe_experts(xs, chunk_e, chunk_row, chunk_tiles, chunk_idx, w_gu, b_gu, w_down, b_down)
    yg = y.at[pos_y.T.reshape(-1)].get(mode="promise_in_bounds")
    return _moe_combine(x, yg, gates, mod, 5, name=name)


def _lambda_init(layer):
    return 0.8 - 0.6 * math.exp(-0.3 * layer)


def kernel(x_prompt, x_sample, cache_l0_k, cache_l0_v, cache_l1_ckv, cache_l1_kpe, c, c_ctx, mod0_w, mod0_b, norm0a_w, in0_w, conv0_w, conv0_b, lam0_q1, lam0_k1, lam0_q2, lam0_k2, subln0_w, out0_w, norm0b_w, moe0_router_w, moe0_router_b, moe0_w_gu, moe0_b_gu, moe0_w_down, moe0_b_down, mod1_w, mod1_b, norm1a_w, dqkv1_w, qnorm1_w, kvnorm1_w, uq1_w, ukv1_w, o1_w, norm1b_w, moe1_router_w, moe1_router_b, moe1_w_gu, moe1_b_gu, moe1_w_down, moe1_b_down, final_norm_w):
    x = jnp.concatenate([x_prompt.reshape(N_CTX, D_MODEL), x_sample.reshape(N_LAT, D_MODEL)], axis=0)
    cond = jnp.zeros((COND_ROWS, D_MODEL), F32).at[0].set(c_ctx).at[1:1 + DEC_BATCH].set(c)
    scond = jax.nn.silu(cond).astype(BF16)
    cos, sin = _rope_tables()

    mod = _mm(scond, mod0_w, tm=COND_ROWS, tn=512, out_dtype=F32,
              bias=mod0_b.reshape(1, -1), name="mod0")
    h = _norm_mod(x, norm0a_w, mod, 0, 1, name="norm0a")
    proj = _mm(h, in0_w, tm=1024, tn=512, out_dtype=F32, name="in0_proj")
    qk_rot = _rope_latent(proj, cos, sin, 0, 2 * A_WIDTH, name="rope0")
    lamv = jnp.stack([lam0_q1, lam0_k1, lam0_q2, lam0_k2])
    subw = subln0_w.reshape(1, A_V_DIM)
    li = _lambda_init(0)
    o_ctx, state_l0_k, state_l0_v = _diff_ctx(proj, lamv, subw, li)
    o_lat = _diff_lat(qk_rot, proj, cache_l0_k, cache_l0_v, lamv, subw, li)
    z_ctx = _short_conv(proj, conv0_w, conv0_b, 0, N_CTX, SEQ, name="conv_ctx")
    z_lat = _short_conv(proj, conv0_w, conv0_b, N_CTX, N_LAT, DEC_SEQ, name="conv_lat")
    mix = jnp.concatenate([jnp.concatenate([o_ctx, o_lat], axis=0),
                           jnp.concatenate([z_ctx, z_lat], axis=0)], axis=1)
    x = _mm(mix, out0_w, tm=1024, tn=512, out_dtype=F32, resid=x, gate=mod, gate_chunk=2, name="out0_proj")
    routed = _norm_mod(x, norm0b_w, mod, 3, 4, moe0_router_w, moe0_router_b, name="norm0b")
    x = _moe(x, routed, mod, moe0_w_gu, moe0_b_gu, moe0_w_down, moe0_b_down, name="moe0_combine")

    mod = _mm(scond, mod1_w, tm=COND_ROWS, tn=512, out_dtype=F32,
              bias=mod1_b.reshape(1, -1), name="mod1")
    h = _norm_mod(x, norm1a_w, mod, 0, 1, name="norm1a")
    proj = _mm(h, dqkv1_w, tm=1024, tn=256, out_dtype=F32, name="dqkv_proj")
    cq = _rmsnorm_cols(proj, qnorm1_w, 0, Q_LORA, BF16, name="q_norm")
    ckv = _rmsnorm_cols(proj, kvnorm1_w, Q_LORA // KV_LORA, KV_LORA, F32, name="kv_norm")
    kpe = proj[:, Q_LORA + KV_LORA:]
    uq = uq1_w.reshape(Q_LORA, C_HEADS, QK_NOPE + QK_ROPE)
    uq = jnp.concatenate([uq[:, :, :QK_NOPE].reshape(Q_LORA, -1), uq[:, :, QK_NOPE:].reshape(Q_LORA, -1)], axis=1)
    q_all = _mm(cq, uq, tm=1024, tn=512, out_dtype=F32, name="uq_proj")
    qpe_rot = _rope_latent(q_all, cos, sin, C_HEADS * QK_NOPE, C_HEADS * QK_ROPE, name="rope1_q")
    kpe_lat = kpe[N_CTX:]
    kpe_rot = _rope_kpe(jnp.concatenate([kpe_lat, kpe_lat], axis=1), cos, sin, name="rope1_k")[:, :QK_ROPE]
    kpe_lat_all = jnp.concatenate([cache_l1_kpe, kpe_rot.reshape(DEC_BATCH, DEC_SEQ, QK_ROPE)], axis=1)
    ckv_lat_all = jnp.concatenate([cache_l1_ckv, ckv[N_CTX:].reshape(DEC_BATCH, DEC_SEQ, KV_LORA)], axis=1)
    kv_ctx = _mm(ckv[:N_CTX].astype(BF16), ukv1_w, tm=1024, tn=512, out_dtype=BF16, name="ukv_ctx")
    kv_lat = _mm(ckv_lat_all.reshape(-1, KV_LORA).astype(BF16), ukv1_w, tm=1280, tn=512,
                 out_dtype=BF16, name="ukv_lat")
    o_ctx = _mla_ctx(q_all, kv_ctx, kpe)
    o_lat = _mla_lat(q_all, qpe_rot, kv_lat, kpe_lat_all)
    mix = jnp.concatenate([o_ctx, o_lat], axis=0)
    x = _mm(mix, o1_w, tm=1024, tn=512, out_dtype=F32, resid=x, gate=mod, gate_chunk=2, name="o1_proj")
    routed = _norm_mod(x, norm1b_w, mod, 3, 4, moe1_router_w, moe1_router_b, name="norm1b")
    x = _moe(x, routed, mod, moe1_w_gu, moe1_b_gu, moe1_w_down, moe1_b_down, name="moe1_combine")

    y = _rmsnorm_cols(x, final_norm_w, 0, D_MODEL, F32, name="final_norm")
    state_l1_ckv = ckv[:N_CTX].reshape(BATCH, SEQ, KV_LORA)
    state_l1_kpe = kpe[:N_CTX].reshape(BATCH, SEQ, QK_ROPE)
    return (y[:N_CTX].reshape(BATCH, SEQ, D_MODEL), y[N_CTX:].reshape(DEC_BATCH, DEC_SEQ, D_MODEL),
            state_l0_k, state_l0_v, state_l1_ckv, state_l1_kpe)
```

```python
import functools
import math

import jax
import jax.numpy as jnp
from jax import lax
from jax.experimental import pallas as pl
from jax.experimental.pallas import tpu as pltpu

F32 = jnp.float32
BF16 = jnp.bfloat16

D_MODEL = 4096
BATCH = 16
SEQ = 256
DEC_BATCH = 2
DEC_SEQ = 1024
PAST_LEN = 256
N_CTX = BATCH * SEQ
N_LAT = DEC_BATCH * DEC_SEQ
N_TOK = N_CTX + N_LAT
GRID_W = 64
NORM_EPS = 1e-6
ROPE_BASE = 10000.0
N_MOD = 6
A_HEADS = 16
A_QK_DIM = 64
A_V_DIM = 128
A_WIDTH = A_HEADS * A_V_DIM
B_WIDTH = D_MODEL - A_WIDTH
C_HEADS = 32
Q_LORA = 1024
KV_LORA = 512
QK_NOPE = 128
QK_ROPE = 64
V_DIM = 128
N_EXPERTS = 32
TOP_K = 4
D_FF = 2048
SWIGLU_ALPHA = 1.702
SWIGLU_LIMIT = 7.0

LANE = 128
COND_ROWS = 16
ATTN_HB = 4
VMEM_LIMIT = 56 * 1024 * 1024
MOE_VMEM_LIMIT = 60 * 1024 * 1024
TM_MOE = 128
R_CAP = 1024
TILES_PER_CHUNK = R_CAP // TM_MOE
MOE_ROW_STEPS = (768, 896, R_CAP)
N_CHUNKS = N_EXPERTS + (N_TOK * TOP_K) // R_CAP
PX_MOE = N_TOK * TOP_K + N_EXPERTS * TM_MOE + R_CAP
PY_MOE = N_CHUNKS * R_CAP

_NT_DIMS = (((1,), (1,)), ((), ()))


def _cparams(n_axes):
    return pltpu.CompilerParams(dimension_semantics=("arbitrary",) * n_axes,
                                vmem_limit_bytes=VMEM_LIMIT)


def _group_of_row(r0):
    return jnp.where(r0 < N_CTX, 0, 1 + (r0 - N_CTX) // DEC_SEQ)


def _mm_kernel(*refs, mode, tm):
    if mode == "plain":
        x_ref, w_ref, o_ref = refs
    elif mode == "bias":
        x_ref, w_ref, b_ref, o_ref = refs
    else:
        x_ref, w_ref, g_ref, r_ref, o_ref = refs
    acc = jnp.dot(x_ref[...], w_ref[...].astype(BF16), preferred_element_type=F32)
    if mode == "bias":
        acc = acc + b_ref[...]
    elif mode == "resid":
        g = _group_of_row(pl.program_id(0) * tm)
        acc = r_ref[...] + g_ref[pl.ds(g, 1), :] * acc
    o_ref[...] = acc.astype(o_ref.dtype)


def _mm(x, w, *, tm, tn, out_dtype, bias=None, resid=None, gate=None, gate_chunk=0, name):
    m, k = x.shape
    n = w.shape[1]
    assert m % tm == 0
    grid = (m // tm, pl.cdiv(n, tn))
    in_specs = [pl.BlockSpec((tm, k), lambda i, j: (i, 0)),
                pl.BlockSpec((k, tn), lambda i, j: (0, j))]
    args = [x, w]
    mode = "plain"
    if bias is not None:
        mode = "bias"
        in_specs.append(pl.BlockSpec((1, tn), lambda i, j: (0, j)))
        args.append(bias)
    elif resid is not None:
        mode = "resid"
        off = gate_chunk * (D_MODEL // tn)
        in_specs.append(pl.BlockSpec((gate.shape[0], tn), lambda i, j: (0, off + j)))
        in_specs.append(pl.BlockSpec((tm, tn), lambda i, j: (i, j)))
        args += [gate, resid]
    return pl.pallas_call(
        functools.partial(_mm_kernel, mode=mode, tm=tm),
        grid=grid, in_specs=in_specs,
        out_specs=pl.BlockSpec((tm, tn), lambda i, j: (i, j)),
        out_shape=jax.ShapeDtypeStruct((m, n), out_dtype),
        compiler_params=_cparams(2), name=name)(*args)


def _rms(x, w):
    return x * lax.rsqrt(jnp.mean(x * x, axis=-1, keepdims=True) + NORM_EPS) * w


def _top_k_gates(logits):
    lane = lax.broadcasted_iota(jnp.int32, logits.shape, 1)
    vals, idxs = [], []
    for _ in range(TOP_K):
        m = logits.max(axis=-1, keepdims=True)
        idx = jnp.min(jnp.where(logits == m, lane, N_EXPERTS), axis=-1, keepdims=True)
        vals.append(m)
        idxs.append(idx)
        logits = jnp.where(lane == idx, -jnp.inf, logits)
    exps = [jnp.exp(v - vals[0]) for v in vals]
    denom = exps[0]
    for e in exps[1:]:
        denom = denom + e
    return idxs, [e / denom for e in exps]


def _norm_mod_kernel(x_ref, w_ref, sh_ref, sc_ref, *rest, tm, with_router):
    g = _group_of_row(pl.program_id(0) * tm)
    y = _rms(x_ref[...], w_ref[...])
    h = y * (1.0 + sc_ref[pl.ds(g, 1), :]) + sh_ref[pl.ds(g, 1), :]
    if not with_router:
        (h_ref,) = rest
        h_ref[...] = h.astype(BF16)
        return
    rw_ref, rb_ref, hp_ref, ti_ref, gt_ref, rk_ref, cnt_ref, seen = rest
    logits = jnp.dot(h, rw_ref[...], preferred_element_type=F32,
                     precision=lax.Precision.HIGHEST) + rb_ref[...]
    idxs, gates = _top_k_gates(logits)

    @pl.when(pl.program_id(0) == 0)
    def _():
        seen[...] = jnp.zeros_like(seen)

    before = (lax.broadcasted_iota(jnp.int32, (tm, tm), 1)
              < lax.broadcasted_iota(jnp.int32, (tm, tm), 0)).astype(BF16)
    lane = lax.broadcasted_iota(jnp.int32, (tm, N_EXPERTS), 1)
    count = seen[...]
    for k in range(TOP_K):
        onehot = (lane == idxs[k]).astype(F32)
        earlier = jnp.dot(before, onehot.astype(BF16), preferred_element_type=F32) + count
        rk_ref[:, k:k + 1] = jnp.sum(onehot * earlier, axis=-1, keepdims=True).astype(jnp.int32)
        count = count + jnp.sum(onehot, axis=0, keepdims=True)
        ti_ref[:, k:k + 1] = idxs[k]
        gt_ref[:, k:k + 1] = gates[k]
    seen[...] = count
    cnt_ref[...] = count.astype(jnp.int32)
    hp_ref[...] = h


def _norm_mod(x, norm_w, mod, sh_chunk, sc_chunk, router_w=None, router_b=None, *, name):
    tm = 256
    with_router = router_w is not None
    in_specs = [pl.BlockSpec((tm, D_MODEL), lambda i: (i, 0)),
                pl.BlockSpec((1, D_MODEL), lambda i: (0, 0)),
                pl.BlockSpec((COND_ROWS, D_MODEL), lambda i: (0, sh_chunk)),
                pl.BlockSpec((COND_ROWS, D_MODEL), lambda i: (0, sc_chunk))]
    args = [x, norm_w.reshape(1, D_MODEL), mod, mod]
    if with_router:
        in_specs += [pl.BlockSpec((D_MODEL, N_EXPERTS), lambda i: (0, 0)),
                     pl.BlockSpec((1, N_EXPERTS), lambda i: (0, 0))]
        args += [router_w, router_b.reshape(1, N_EXPERTS)]
        slot = pl.BlockSpec((tm, TOP_K), lambda i: (i, 0))
        out_specs = [pl.BlockSpec((tm, D_MODEL), lambda i: (i, 0)), slot, slot, slot,
                     pl.BlockSpec((1, N_EXPERTS), lambda i: (0, 0))]
        out_shape = [jax.ShapeDtypeStruct((N_TOK, D_MODEL), F32),
                     jax.ShapeDtypeStruct((N_TOK, TOP_K), jnp.int32),
                     jax.ShapeDtypeStruct((N_TOK, TOP_K), F32),
                     jax.ShapeDtypeStruct((N_TOK, TOP_K), jnp.int32),
                     jax.ShapeDtypeStruct((1, N_EXPERTS), jnp.int32)]
        scratch = [pltpu.VMEM((1, N_EXPERTS), F32)]
    else:
        out_specs = [pl.BlockSpec((tm, D_MODEL), lambda i: (i, 0))]
        out_shape = [jax.ShapeDtypeStruct((N_TOK, D_MODEL), BF16)]
        scratch = []
    out = pl.pallas_call(
        functools.partial(_norm_mod_kernel, tm=tm, with_router=with_router),
        grid=(N_TOK // tm,), in_specs=in_specs, out_specs=out_specs, out_shape=out_shape,
        scratch_shapes=scratch, compiler_params=_cparams(1), name=name)(*args)
    return out if with_router else out[0]


def _rms_kernel(x_ref, w_ref, o_ref):
    o_ref[...] = _rms(x_ref[...], w_ref[...]).astype(o_ref.dtype)


def _rmsnorm_cols(x, w, col_block, width, out_dtype, *, name):
    tm = 256
    m = x.shape[0]
    return pl.pallas_call(
        _rms_kernel, grid=(m // tm,),
        in_specs=[pl.BlockSpec((tm, width), lambda i: (i, col_block)),
                  pl.BlockSpec((1, width), lambda i: (0, 0))],
        out_specs=pl.BlockSpec((tm, width), lambda i: (i, 0)),
        out_shape=jax.ShapeDtypeStruct((m, width), out_dtype),
        compiler_params=_cparams(1), name=name)(x, w.reshape(1, width))


def _rope_tables():
    t = jnp.arange(DEC_SEQ, dtype=jnp.int32)
    row = (t // GRID_W).astype(F32)[:, None]
    col = (t % GRID_W).astype(F32)[:, None]
    half = QK_ROPE // 2
    inv = ROPE_BASE ** (-jnp.arange(0, half, 2, dtype=F32) / half)[None, :]
    ar, ac = row * inv, col * inv
    cos64 = jnp.concatenate([jnp.cos(ar), jnp.cos(ar), jnp.cos(ac), jnp.cos(ac)], axis=-1)
    sin64 = jnp.concatenate([-jnp.sin(ar), jnp.sin(ar), -jnp.sin(ac), jnp.sin(ac)], axis=-1)
    return jnp.tile(cos64, (1, 2)), jnp.tile(sin64, (1, 2))


def _swap16(x):
    lane = lax.broadcasted_iota(jnp.int32, x.shape, x.ndim - 1)
    return jnp.where((lane & 16) == 0,
                     pltpu.roll(x, x.shape[-1] - 16, x.ndim - 1),
                     pltpu.roll(x, 16, x.ndim - 1))


def _rope_kernel(x_ref, cos_ref, sin_ref, o_ref, *, n_chunks):
    cos = cos_ref[...]
    sin = sin_ref[...]
    for c in range(n_chunks):
        x = x_ref[:, c * LANE:(c + 1) * LANE]
        o_ref[:, c * LANE:(c + 1) * LANE] = (x * cos + _swap16(x) * sin).astype(o_ref.dtype)


def _rope_latent(x, cos, sin, col0, width, *, name):
    tm, tc = 256, 2048
    assert width % tc == 0 and col0 % tc == 0
    r0, c0 = N_CTX // tm, col0 // tc
    per_seq = DEC_SEQ // tm
    return pl.pallas_call(
        functools.partial(_rope_kernel, n_chunks=tc // LANE),
        grid=(N_LAT // tm, width // tc),
        in_specs=[pl.BlockSpec((tm, tc), lambda i, j: (r0 + i, c0 + j)),
                  pl.BlockSpec((tm, LANE), lambda i, j: (i % per_seq, 0)),
                  pl.BlockSpec((tm, LANE), lambda i, j: (i % per_seq, 0))],
        out_specs=pl.BlockSpec((tm, tc), lambda i, j: (i, j)),
        out_shape=jax.ShapeDtypeStruct((N_LAT, width), BF16),
        compiler_params=_cparams(2), name=name)(x, cos, sin)


def _rope_kpe_kernel(x_ref, cos_ref, sin_ref, o_ref):
    x = x_ref[...]
    o_ref[...] = x * cos_ref[...] + _swap16(x) * sin_ref[...]


def _rope_kpe(kpe_pad, cos, sin, *, name):
    tm = 256
    per_seq = DEC_SEQ // tm
    return pl.pallas_call(
        _rope_kpe_kernel, grid=(N_LAT // tm,),
        in_specs=[pl.BlockSpec((tm, LANE), lambda i: (i, 0)),
                  pl.BlockSpec((tm, LANE), lambda i: (i % per_seq, 0)),
                  pl.BlockSpec((tm, LANE), lambda i: (i % per_seq, 0))],
        out_specs=pl.BlockSpec((tm, LANE), lambda i: (i, 0)),
        out_shape=jax.ShapeDtypeStruct((N_LAT, LANE), F32),
        compiler_params=_cparams(1), name=name)(kpe_pad, cos, sin)


def _softmax_pv(s_parts, v_parts):
    m = s_parts[0].max(axis=-1, keepdims=True)
    for s in s_parts[1:]:
        m = jnp.maximum(m, s.max(axis=-1, keepdims=True))
    acc, l = None, None
    for s, v in zip(s_parts, v_parts):
        p = jnp.exp(s - m)
        ps = p.sum(axis=-1, keepdims=True)
        pv = jnp.dot(p.astype(BF16), v, preferred_element_type=F32)
        acc = pv if acc is None else acc + pv
        l = ps if l is None else l + ps
    return acc / l


def _diff_lambda(lam_ref, lambda_init):
    lv = lam_ref[...]
    s1 = jnp.sum(lv[0:1] * lv[1:2], axis=-1, keepdims=True)
    s2 = jnp.sum(lv[2:3] * lv[3:4], axis=-1, keepdims=True)
    return jnp.exp(s1) - jnp.exp(s2) + lambda_init


def _diff_finish(o1, o2, lam, subw, lambda_init):
    o = o1 - lam * o2
    return _rms(o, subw) * (1.0 - lambda_init)


def _diff_ctx_kernel(lam_ref, q_ref, k_ref, v_ref, subw_ref, o_ref, ks_ref, vs_ref, *, lambda_init):
    dk = A_QK_DIM
    lam = _diff_lambda(lam_ref, lambda_init)
    for h in range(ATTN_HB):
        cols = slice(h * LANE, (h + 1) * LANE)
        k = k_ref[:, cols]
        v = v_ref[:, cols]
        ks_ref[h] = k
        vs_ref[h] = v
        q = (q_ref[:, cols] * (dk ** -0.5)).astype(BF16)
        kb = k.astype(BF16)
        vb = v.astype(BF16)
        outs = []
        for lo in (0, dk):
            s = lax.dot_general(q[:, lo:lo + dk], kb[:, lo:lo + dk], _NT_DIMS, preferred_element_type=F32)
            outs.append(_softmax_pv([s], [vb]))
        o_ref[:, cols] = _diff_finish(outs[0], outs[1], lam, subw_ref[...], lambda_init).astype(BF16)


def _diff_ctx(proj, lamv, subw, lambda_init):
    w = ATTN_HB * LANE
    kc, vc = A_WIDTH // w, 2 * A_WIDTH // w
    blk = lambda off: pl.BlockSpec((SEQ, w), lambda b, h: (b, off + h))
    st = pl.BlockSpec((None, ATTN_HB, SEQ, LANE), lambda b, h: (b, h, 0, 0))
    return pl.pallas_call(
        functools.partial(_diff_ctx_kernel, lambda_init=lambda_init),
        grid=(BATCH, A_HEADS // ATTN_HB),
        in_specs=[pl.BlockSpec((4, A_QK_DIM), lambda b, h: (0, 0)),
                  blk(0), blk(kc), blk(vc),
                  pl.BlockSpec((1, A_V_DIM), lambda b, h: (0, 0))],
        out_specs=[pl.BlockSpec((SEQ, w), lambda b, h: (b, h)), st, st],
        out_shape=[jax.ShapeDtypeStruct((N_CTX, A_WIDTH), BF16),
                   jax.ShapeDtypeStruct((BATCH, A_HEADS, SEQ, 2 * A_QK_DIM), F32),
                   jax.ShapeDtypeStruct((BATCH, A_HEADS, SEQ, A_V_DIM), F32)],
        compiler_params=_cparams(2), name="diff_attn_ctx")(lamv, proj, proj, proj, subw)


def _diff_lat_kernel(lam_ref, q_ref, k_ref, v_ref, ck_ref, cv_ref, subw_ref, o_ref, *, lambda_init):
    dk = A_QK_DIM
    lam = _diff_lambda(lam_ref, lambda_init)
    for h in range(ATTN_HB):
        cols = slice(h * LANE, (h + 1) * LANE)
        q = q_ref[:, cols] * (dk ** -0.5)
        kb = k_ref[:, cols]
        ckb = ck_ref[h].astype(BF16)
        vb = v_ref[:, cols].astype(BF16)
        cvb = cv_ref[h].astype(BF16)
        outs = []
        for lo in (0, dk):
            qh = q[:, lo:lo + dk]
            sc = lax.dot_general(qh, ckb[:, lo:lo + dk], _NT_DIMS, preferred_element_type=F32)
            sn = lax.dot_general(qh, kb[:, lo:lo + dk], _NT_DIMS, preferred_element_type=F32)
            outs.append(_softmax_pv([sc, sn], [cvb, vb]))
        o_ref[:, cols] = _diff_finish(outs[0], outs[1], lam, subw_ref[...], lambda_init).astype(BF16)


def _diff_lat(qk_rot, proj, cache_k, cache_v, lamv, subw, lambda_init):
    tq = 256
    nq = DEC_SEQ // tq
    w = ATTN_HB * LANE
    kc, vc = A_WIDTH // w, 2 * A_WIDTH // w
    v_row0 = N_CTX // DEC_SEQ
    cache = pl.BlockSpec((None, ATTN_HB, PAST_LEN, LANE), lambda b, h, i: (b, h, 0, 0))
    return pl.pallas_call(
        functools.partial(_diff_lat_kernel, lambda_init=lambda_init),
        grid=(DEC_BATCH, A_HEADS // ATTN_HB, nq),
        in_specs=[pl.BlockSpec((4, A_QK_DIM), lambda b, h, i: (0, 0)),
                  pl.BlockSpec((tq, w), lambda b, h, i: (b * nq + i, h)),
                  pl.BlockSpec((DEC_SEQ, w), lambda b, h, i: (b, kc + h)),
                  pl.BlockSpec((DEC_SEQ, w), lambda b, h, i: (v_row0 + b, vc + h)),
                  cache, cache,
                  pl.BlockSpec((1, A_V_DIM), lambda b, h, i: (0, 0))],
        out_specs=pl.BlockSpec((tq, w), lambda b, h, i: (b * nq + i, h)),
        out_shape=jax.ShapeDtypeStruct((N_LAT, A_WIDTH), BF16),
        compiler_params=_cparams(3), name="diff_attn_lat")(
            lamv, qk_rot, qk_rot, proj, cache_k, cache_v, subw)


def _mla_kernel(qn_ref, qp_ref, kv_ref, kpe_ref, o_ref):
    scale = (QK_NOPE + QK_ROPE) ** -0.5
    kpe = kpe_ref[...].astype(BF16)
    for h in range(ATTN_HB):
        qn = (qn_ref[:, h * LANE:(h + 1) * LANE] * scale).astype(BF16)
        qp = (qp_ref[:, h * QK_ROPE:(h + 1) * QK_ROPE].astype(F32) * scale).astype(BF16)
        kn = kv_ref[:, 2 * h * LANE:(2 * h + 1) * LANE]
        v = kv_ref[:, (2 * h + 1) * LANE:(2 * h + 2) * LANE]
        s = lax.dot_general(qn, kn, _NT_DIMS, preferred_element_type=F32)
        s = s + lax.dot_general(qp, kpe, _NT_DIMS, preferred_element_type=F32)
        o_ref[:, h * LANE:(h + 1) * LANE] = _softmax_pv([s], [v]).astype(BF16)


def _mla_ctx(q_all, kv_ctx, kpe_all):
    hg = C_HEADS // ATTN_HB
    pe0 = C_HEADS * QK_NOPE // (ATTN_HB * QK_ROPE)
    return pl.pallas_call(
        _mla_kernel, grid=(BATCH, hg),
        in_specs=[pl.BlockSpec((SEQ, ATTN_HB * LANE), lambda b, h: (b, h)),
                  pl.BlockSpec((SEQ, ATTN_HB * QK_ROPE), lambda b, h: (b, pe0 + h)),
                  pl.BlockSpec((SEQ, 2 * ATTN_HB * LANE), lambda b, h: (b, h)),
                  pl.BlockSpec((SEQ, QK_ROPE), lambda b, h: (b, 0))],
        out_specs=pl.BlockSpec((SEQ, ATTN_HB * LANE), lambda b, h: (b, h)),
        out_shape=jax.ShapeDtypeStruct((N_CTX, C_HEADS * V_DIM), BF16),
        compiler_params=_cparams(2), name="mla_ctx")(q_all, q_all, kv_ctx, kpe_all)


def _mla_lat(q_all, qpe_rot, kv_lat, kpe_lat):
    tq = 256
    nq = DEC_SEQ // tq
    hg = C_HEADS // ATTN_HB
    lk = PAST_LEN + DEC_SEQ
    r0 = N_CTX // tq
    return pl.pallas_call(
        _mla_kernel, grid=(DEC_BATCH, hg, nq),
        in_specs=[pl.BlockSpec((tq, ATTN_HB * LANE), lambda b, h, i: (r0 + b * nq + i, h)),
                  pl.BlockSpec((tq, ATTN_HB * QK_ROPE), lambda b, h, i: (b * nq + i, h)),
                  pl.BlockSpec((lk, 2 * ATTN_HB * LANE), lambda b, h, i: (b, h)),
                  pl.BlockSpec((None, lk, QK_ROPE), lambda b, h, i: (b, 0, 0))],
        out_specs=pl.BlockSpec((tq, ATTN_HB * LANE), lambda b, h, i: (b * nq + i, h)),
        out_shape=jax.ShapeDtypeStruct((N_LAT, C_HEADS * V_DIM), BF16),
        compiler_params=_cparams(3), name="mla_lat")(q_all, qpe_rot, kv_lat, kpe_lat)


def _conv_kernel(gb_ref, gc_ref, x_ref, cw_ref, cb_ref, z_ref, *, seq):
    u = gc_ref[...] * x_ref[...]
    row = lax.broadcasted_iota(jnp.int32, u.shape, 0)
    prev = jnp.where(row == 0, 0.0, pltpu.roll(u, 1, 0))
    nxt = jnp.where(row == seq - 1, 0.0, pltpu.roll(u, seq - 1, 0))
    cw = cw_ref[...]
    y = prev * cw[0:1] + u * cw[1:2] + nxt * cw[2:3] + cb_ref[...]
    z_ref[...] = (gb_ref[...] * y).astype(BF16)


def _short_conv(proj, conv_w, conv_b, row0, n_rows, seq, *, name):
    tc = 512
    base = 3 * A_WIDTH // tc
    nb = B_WIDTH // tc
    r0 = row0 // seq
    blk = lambda off: pl.BlockSpec((seq, tc), lambda i, j: (r0 + i, base + off * nb + j))
    return pl.pallas_call(
        functools.partial(_conv_kernel, seq=seq),
        grid=(n_rows // seq, nb),
        in_specs=[blk(0), blk(1), blk(2),
                  pl.BlockSpec((3, tc), lambda i, j: (0, j)),
                  pl.BlockSpec((1, tc), lambda i, j: (0, j))],
        out_specs=pl.BlockSpec((seq, tc), lambda i, j: (i, j)),
        out_shape=jax.ShapeDtypeStruct((n_rows, B_WIDTH), BF16),
        compiler_params=_cparams(2), name=name)(proj, proj, proj, conv_w, conv_b.reshape(1, B_WIDTH))


def _moe_kernel(ce_ref, crow_ref, cnt_ref, cidx_ref, xs_hbm, wg_ref, wl_ref, bg_ref, bl_ref, wd_ref, bd_ref,
                y_ref, stage, xb, act, sem, *, nj, ta):
    c = pl.program_id(0)
    s = pl.program_id(1)
    n_tiles = cnt_ref[c]
    valid = n_tiles > 0
    half_rows = R_CAP // 2
    nxt = jnp.minimum(c + 1, N_CHUNKS - 1)
    next_valid = (c + 1 < N_CHUNKS) & (cnt_ref[nxt] > 0)

    def x_copy(ci, half):
        row0 = pl.multiple_of(crow_ref[ci], TM_MOE) + half * half_rows
        return pltpu.make_async_copy(xs_hbm.at[pl.ds(row0, half_rows)], stage, sem)

    def land(half):
        for r in range(half_rows // TM_MOE):
            rows = slice(r * TM_MOE, (r + 1) * TM_MOE)
            xb[half * half_rows + r * TM_MOE:half * half_rows + (r + 1) * TM_MOE, :] = stage[rows, :].astype(BF16)

    @pl.when((c == 0) & (s == 0))
    def _():
        x_copy(0, 0).start()
        x_copy(0, 0).wait()
        land(0)
        x_copy(0, 1).start()

    @pl.when(valid & (s == 0))
    def _():
        x_copy(c, 1).wait()
        land(1)

    @pl.when(valid & next_valid & (s == nj))
    def _():
        x_copy(nxt, 0).start()

    @pl.when(valid & next_valid & (s == nj + 4))
    def _():
        x_copy(nxt, 0).wait()
        land(0)
        x_copy(nxt, 1).start()

    def for_row_count(body):
        lo_tiles = 0
        for m in MOE_ROW_STEPS:
            hi_tiles = m // TM_MOE
            pl.when(valid & (n_tiles > lo_tiles) & (n_tiles <= hi_tiles))(functools.partial(body, m))
            lo_tiles = hi_tiles

    def gate_up(m):
        x = xb[:m, :]
        g = jnp.dot(x, wg_ref[...].astype(BF16), preferred_element_type=F32) + bg_ref[...]
        l = jnp.dot(x, wl_ref[...].astype(BF16), preferred_element_type=F32) + bl_ref[...]
        glu = jnp.minimum(g, SWIGLU_LIMIT)
        lin = jnp.clip(l, -SWIGLU_LIMIT, SWIGLU_LIMIT)
        col0 = pl.multiple_of(jnp.minimum(s, nj - 1) * ta, ta)
        act[:m, pl.ds(col0, ta)] = (glu * jax.nn.sigmoid(SWIGLU_ALPHA * glu) * (lin + 1.0)).astype(BF16)

    def down(m):
        y_ref[:m, :] = jnp.dot(act[:m, :], wd_ref[...].astype(BF16), preferred_element_type=F32) + bd_ref[...]
        if m < R_CAP:
            y_ref[m:, :] = jnp.zeros((R_CAP - m, 2 * ta), F32)

    @pl.when(s < nj)
    def _():
        for_row_count(gate_up)

    @pl.when(s >= nj)
    def _():
        for_row_count(down)


def _moe_experts(xs, chunk_e, chunk_row, chunk_tiles, chunk_idx, w_gu, b_gu, w_down, b_down):
    ta = 256
    nj = D_FF // ta
    nd = D_MODEL // (2 * ta)

    def gu_j(c, s, cnt):
        return jnp.where(cnt[c] > 0, jnp.minimum(s, nj - 1), nj - 1)

    def dn_j(c, s, cnt):
        return jnp.where(cnt[c] > 0, jnp.maximum(s - nj, 0), nd - 1)

    return pl.pallas_call(
        functools.partial(_moe_kernel, nj=nj, ta=ta),
        grid_spec=pltpu.PrefetchScalarGridSpec(
            num_scalar_prefetch=4, grid=(N_CHUNKS, nj + nd),
            in_specs=[
                pl.BlockSpec(memory_space=pl.ANY),
                pl.BlockSpec((None, D_MODEL, ta), lambda c, s, ce, cr, cnt, ci: (ce[c], 0, gu_j(c, s, cnt))),
                pl.BlockSpec((None, D_MODEL, ta), lambda c, s, ce, cr, cnt, ci: (ce[c], 0, nj + gu_j(c, s, cnt))),
                pl.BlockSpec((None, 1, ta), lambda c, s, ce, cr, cnt, ci: (ce[c], 0, gu_j(c, s, cnt))),
                pl.BlockSpec((None, 1, ta), lambda c, s, ce, cr, cnt, ci: (ce[c], 0, nj + gu_j(c, s, cnt))),
                pl.BlockSpec((None, D_FF, 2 * ta), lambda c, s, ce, cr, cnt, ci: (ce[c], 0, dn_j(c, s, cnt))),
                pl.BlockSpec((None, 1, 2 * ta), lambda c, s, ce, cr, cnt, ci: (ce[c], 0, dn_j(c, s, cnt))),
            ],
            out_specs=pl.BlockSpec((R_CAP, 2 * ta), lambda c, s, ce, cr, cnt, ci: (ci[c], dn_j(c, s, cnt))),
            scratch_shapes=[pltpu.VMEM((R_CAP // 2, D_MODEL), F32),
                            pltpu.VMEM((R_CAP, D_MODEL), BF16),
                            pltpu.VMEM((R_CAP, D_FF), BF16),
                            pltpu.SemaphoreType.DMA(())]),
        out_shape=jax.ShapeDtypeStruct((PY_MOE, D_MODEL), F32),
        compiler_params=pltpu.CompilerParams(dimension_semantics=("arbitrary", "arbitrary"),
                                             vmem_limit_bytes=MOE_VMEM_LIMIT),
        name="moe_experts")(
            chunk_e, chunk_row, chunk_tiles, chunk_idx, xs, w_gu, w_gu,
            b_gu.reshape(N_EXPERTS, 1, 2 * D_FF), b_gu.reshape(N_EXPERTS, 1, 2 * D_FF),
            w_down, b_down.reshape(N_EXPERTS, 1, D_MODEL))


def _moe_routing(top_i, rank, counts):
    e_flat = top_i.reshape(-1)
    rank = rank.reshape(-1)
    counts = counts.reshape(-1)
    padded = ((counts + TM_MOE - 1) // TM_MOE) * TM_MOE
    starts = jnp.cumsum(padded) - padded
    n_ch = (counts + R_CAP - 1) // R_CAP
    ch_end = jnp.cumsum(n_ch)
    ch_base = ch_end - n_ch
    n_chunks = ch_end[-1]
    pos_x = starts[e_flat] + rank
    pos_y = (ch_base[e_flat] + rank // R_CAP) * R_CAP + rank % R_CAP
    row_token = jnp.zeros((PX_MOE,), jnp.int32).at[pos_x].set(
        jnp.arange(N_TOK * TOP_K, dtype=jnp.int32) // TOP_K)
    cid = jnp.arange(N_CHUNKS, dtype=jnp.int32)
    cvalid = cid < n_chunks
    cidx = jnp.minimum(cid, n_chunks - 1)
    chunk_e = jnp.sum((ch_end[None, :] <= cidx[:, None]).astype(jnp.int32), axis=1)
    chunk_e = jnp.minimum(chunk_e, N_EXPERTS - 1)
    k = cidx - ch_base[chunk_e]
    chunk_row = starts[chunk_e] + k * R_CAP
    rows_left = jnp.clip(counts[chunk_e] - k * R_CAP, 0, R_CAP)
    chunk_tiles = jnp.where(cvalid, (rows_left + TM_MOE - 1) // TM_MOE, 0).astype(jnp.int32)
    return pos_y.reshape(N_TOK, TOP_K), row_token, chunk_e, chunk_row.astype(jnp.int32), chunk_tiles, cidx


def _moe_combine_kernel(x_ref, y0_ref, y1_ref, y2_ref, y3_ref, gt_ref, g_ref, o_ref, *, tm):
    g = _group_of_row(pl.program_id(0) * tm)
    gates = gt_ref[...]
    acc = None
    for k, y_ref in enumerate((y0_ref, y1_ref, y2_ref, y3_ref)):
        term = gates[:, k:k + 1] * y_ref[...]
        acc = term if acc is None else acc + term
    o_ref[...] = x_ref[...] + g_ref[pl.ds(g, 1), :] * acc


def _moe_combine(x, yg, gates, mod, chunk, *, name):
    tm = 128
    nb = N_TOK // tm
    y_spec = lambda k: pl.BlockSpec((tm, D_MODEL), lambda i: (k * nb + i, 0))
    return pl.pallas_call(
        functools.partial(_moe_combine_kernel, tm=tm), grid=(nb,),
        in_specs=[pl.BlockSpec((tm, D_MODEL), lambda i: (i, 0)),
                  y_spec(0), y_spec(1), y_spec(2), y_spec(3),
                  pl.BlockSpec((tm, TOP_K), lambda i: (i, 0)),
                  pl.BlockSpec((COND_ROWS, D_MODEL), lambda i: (0, chunk))],
        out_specs=pl.BlockSpec((tm, D_MODEL), lambda i: (i, 0)),
        out_shape=jax.ShapeDtypeStruct((N_TOK, D_MODEL), F32),
        compiler_params=_cparams(1), name=name)(x, yg, yg, yg, yg, gates, mod)


def _moe(x, routed, mod, w_gu, b_gu, w_down, b_down, *, name):
    h, top_i, gates, rank, counts = routed
    pos_y, row_token, chunk_e, chunk_row, chunk_tiles, chunk_idx = _moe_routing(top_i, rank, counts)
    xs = h.at[row_token].get(mode="promise_in_bounds")
    y = _moe_experts(xs, chunk_e, chunk_row, chunk_tiles, chunk_idx, w_gu, b_gu, w_down, b_down)
    yg = y.at[pos_y.T.reshape(-1)].get(mode="promise_in_bounds")
    return _moe_combine(x, yg, gates, mod, 5, name=name)


def _lambda_init(layer):
    return 0.8 - 0.6 * math.exp(-0.3 * layer)


def kernel(x_prompt, x_sample, cache_l0_k, cache_l0_v, cache_l1_ckv, cache_l1_kpe, c, c_ctx, mod0_w, mod0_b, norm0a_w, in0_w, conv0_w, conv0_b, lam0_q1, lam0_k1, lam0_q2, lam0_k2, subln0_w, out0_w, norm0b_w, moe0_router_w, moe0_router_b, moe0_w_gu, moe0_b_gu, moe0_w_down, moe0_b_down, mod1_w, mod1_b, norm1a_w, dqkv1_w, qnorm1_w, kvnorm1_w, uq1_w, ukv1_w, o1_w, norm1b_w, moe1_router_w, moe1_router_b, moe1_w_gu, moe1_b_gu, moe1_w_down, moe1_b_down, final_norm_w):
    x = jnp.concatenate([x_prompt.reshape(N_CTX, D_MODEL), x_sample.reshape(N_LAT, D_MODEL)], axis=0)
    cond = jnp.zeros((COND_ROWS, D_MODEL), F32).at[0].set(c_ctx).at[1:1 + DEC_BATCH].set(c)
    scond = jax.nn.silu(cond).astype(BF16)
    cos, sin = _rope_tables()

    mod = _mm(scond, mod0_w, tm=COND_ROWS, tn=512, out_dtype=F32,
              bias=mod0_b.reshape(1, -1), name="mod0")
    h = _norm_mod(x, norm0a_w, mod, 0, 1, name="norm0a")
    proj = _mm(h, in0_w, tm=1024, tn=512, out_dtype=F32, name="in0_proj")
    qk_rot = _rope_latent(proj, cos, sin, 0, 2 * A_WIDTH, name="rope0")
    lamv = jnp.stack([lam0_q1, lam0_k1, lam0_q2, lam0_k2])
    subw = subln0_w.reshape(1, A_V_DIM)
    li = _lambda_init(0)
    o_ctx, state_l0_k, state_l0_v = _diff_ctx(proj, lamv, subw, li)
    o_lat = _diff_lat(qk_rot, proj, cache_l0_k, cache_l0_v, lamv, subw, li)
    z_ctx = _short_conv(proj, conv0_w, conv0_b, 0, N_CTX, SEQ, name="conv_ctx")
    z_lat = _short_conv(proj, conv0_w, conv0_b, N_CTX, N_LAT, DEC_SEQ, name="conv_lat")
    mix = jnp.concatenate([jnp.concatenate([o_ctx, o_lat], axis=0),
                           jnp.concatenate([z_ctx, z_lat], axis=0)], axis=1)
    x = _mm(mix, out0_w, tm=1024, tn=512, out_dtype=F32, resid=x, gate=mod, gate_chunk=2, name="out0_proj")
    routed = _norm_mod(x, norm0b_w, mod, 3, 4, moe0_router_w, moe0_router_b, name="norm0b")
    x = _moe(x, routed, mod, moe0_w_gu, moe0_b_gu, moe0_w_down, moe0_b_down, name="moe0_combine")

    mod = _mm(scond, mod1_w, tm=COND_ROWS, tn=512, out_dtype=F32,
              bias=mod1_b.reshape(1, -1), name="mod1")
    h = _norm_mod(x, norm1a_w, mod, 0, 1, name="norm1a")
    proj = _mm(h, dqkv1_w, tm=1024, tn=256, out_dtype=F32, name="dqkv_proj")
    cq = _rmsnorm_cols(proj, qnorm1_w, 0, Q_LORA, BF16, name="q_norm")
    ckv = _rmsnorm_cols(proj, kvnorm1_w, Q_LORA // KV_LORA, KV_LORA, F32, name="kv_norm")
    kpe = proj[:, Q_LORA + KV_LORA:]
    uq = uq1_w.reshape(Q_LORA, C_HEADS, QK_NOPE + QK_ROPE)
    uq = jnp.concatenate([uq[:, :, :QK_NOPE].reshape(Q_LORA, -1), uq[:, :, QK_NOPE:].reshape(Q_LORA, -1)], axis=1)
    q_all = _mm(cq, uq, tm=1024, tn=512, out_dtype=F32, name="uq_proj")
    qpe_rot = _rope_latent(q_all, cos, sin, C_HEADS * QK_NOPE, C_HEADS * QK_ROPE, name="rope1_q")
    kpe_lat = kpe[N_CTX:]
    kpe_rot = _rope_kpe(jnp.concatenate([kpe_lat, kpe_lat], axis=1), cos, sin, name="rope1_k")[:, :QK_ROPE]
    kpe_lat_all = jnp.concatenate([cache_l1_kpe, kpe_rot.reshape(DEC_BATCH, DEC_SEQ, QK_ROPE)], axis=1)
    ckv_lat_all = jnp.concatenate([cache_l1_ckv, ckv[N_CTX:].reshape(DEC_BATCH, DEC_SEQ, KV_LORA)], axis=1)
    kv_ctx = _mm(ckv[:N_CTX].astype(BF16), ukv1_w, tm=1024, tn=512, out_dtype=BF16, name="ukv_ctx")
    kv_lat = _mm(ckv_lat_all.reshape(-1, KV_LORA).astype(BF16), ukv1_w, tm=1280, tn=512,
                 out_dtype=BF16, name="ukv_lat")
    o_ctx = _mla_ctx(q_all, kv_ctx, kpe)
    o_lat = _mla_lat(q_all, qpe_rot, kv_lat, kpe_lat_all)
    mix = jnp.concatenate([o_ctx, o_lat], axis=0)
    x = _mm(mix, o1_w, tm=1024, tn=512, out_dtype=F32, resid=x, gate=mod, gate_chunk=2, name="o1_proj")
    routed = _norm_mod(x, norm1b_w, mod, 3, 4, moe1_router_w, moe1_router_b, name="norm1b")
    x = _moe(x, routed, mod, moe1_w_gu, moe1_b_gu, moe1_w_down, moe1_b_down, name="moe1_combine")

    y = _rmsnorm_cols(x, final_norm_w, 0, D_MODEL, F32, name="final_norm")
    state_l1_ckv = ckv[:N_CTX].reshape(BATCH, SEQ, KV_LORA)
    state_l1_kpe = kpe[:N_CTX].reshape(BATCH, SEQ, QK_ROPE)
    return (y[:N_CTX].reshape(BATCH, SEQ, D_MODEL), y[N_CTX:].reshape(DEC_BATCH, DEC_SEQ, D_MODEL),
            state_l0_k, state_l0_v, state_l1_ckv, state_l1_kpe)
```

```python
import functools
import math

import jax
import jax.numpy as jnp
from jax import lax
from jax.experimental import pallas as pl
from jax.experimental.pallas import tpu as pltpu

F32 = jnp.float32
BF16 = jnp.bfloat16

D_MODEL = 4096
BATCH = 16
SEQ = 256
DEC_BATCH = 2
DEC_SEQ = 1024
PAST_LEN = 256
N_CTX = BATCH * SEQ
N_LAT = DEC_BATCH * DEC_SEQ
N_TOK = N_CTX + N_LAT
GRID_W = 64
NORM_EPS = 1e-6
ROPE_BASE = 10000.0
N_MOD = 6
A_HEADS = 16
A_QK_DIM = 64
A_V_DIM = 128
A_WIDTH = A_HEADS * A_V_DIM
B_WIDTH = D_MODEL - A_WIDTH
C_HEADS = 32
Q_LORA = 1024
KV_LORA = 512
QK_NOPE = 128
QK_ROPE = 64
V_DIM = 128
N_EXPERTS = 32
TOP_K = 4
D_FF = 2048
SWIGLU_ALPHA = 1.702
SWIGLU_LIMIT = 7.0

LANE = 128
COND_ROWS = 16
ATTN_HB = 4
VMEM_LIMIT = 56 * 1024 * 1024
MOE_VMEM_LIMIT = 60 * 1024 * 1024
TM_MOE = 128
R_CAP = 1024
TILES_PER_CHUNK = R_CAP // TM_MOE
MOE_ROW_STEPS = (768, 896, R_CAP)
GU_SLOTS = 3
N_CHUNKS = N_EXPERTS + (N_TOK * TOP_K) // R_CAP
PX_MOE = N_TOK * TOP_K + N_EXPERTS * TM_MOE + R_CAP
PY_MOE = N_CHUNKS * R_CAP

_NT_DIMS = (((1,), (1,)), ((), ()))


def _cparams(n_axes):
    return pltpu.CompilerParams(dimension_semantics=("arbitrary",) * n_axes,
                                vmem_limit_bytes=VMEM_LIMIT)


def _group_of_row(r0):
    return jnp.where(r0 < N_CTX, 0, 1 + (r0 - N_CTX) // DEC_SEQ)


def _mm_kernel(*refs, mode, tm):
    if mode == "plain":
        x_ref, w_ref, o_ref = refs
    elif mode == "bias":
        x_ref, w_ref, b_ref, o_ref = refs
    else:
        x_ref, w_ref, g_ref, r_ref, o_ref = refs
    acc = jnp.dot(x_ref[...], w_ref[...].astype(BF16), preferred_element_type=F32)
    if mode == "bias":
        acc = acc + b_ref[...]
    elif mode == "resid":
        g = _group_of_row(pl.program_id(0) * tm)
        acc = r_ref[...] + g_ref[pl.ds(g, 1), :] * acc
    o_ref[...] = acc.astype(o_ref.dtype)


def _mm(x, w, *, tm, tn, out_dtype, bias=None, resid=None, gate=None, gate_chunk=0, name):
    m, k = x.shape
    n = w.shape[1]
    assert m % tm == 0
    grid = (m // tm, pl.cdiv(n, tn))
    in_specs = [pl.BlockSpec((tm, k), lambda i, j: (i, 0)),
                pl.BlockSpec((k, tn), lambda i, j: (0, j))]
    args = [x, w]
    mode = "plain"
    if bias is not None:
        mode = "bias"
        in_specs.append(pl.BlockSpec((1, tn), lambda i, j: (0, j)))
        args.append(bias)
    elif resid is not None:
        mode = "resid"
        off = gate_chunk * (D_MODEL // tn)
        in_specs.append(pl.BlockSpec((gate.shape[0], tn), lambda i, j: (0, off + j)))
        in_specs.append(pl.BlockSpec((tm, tn), lambda i, j: (i, j)))
        args += [gate, resid]
    return pl.pallas_call(
        functools.partial(_mm_kernel, mode=mode, tm=tm),
        grid=grid, in_specs=in_specs,
        out_specs=pl.BlockSpec((tm, tn), lambda i, j: (i, j)),
        out_shape=jax.ShapeDtypeStruct((m, n), out_dtype),
        compiler_params=_cparams(2), name=name)(*args)


def _rms(x, w):
    return x * lax.rsqrt(jnp.mean(x * x, axis=-1, keepdims=True) + NORM_EPS) * w


def _top_k_gates(logits):
    lane = lax.broadcasted_iota(jnp.int32, logits.shape, 1)
    vals, idxs = [], []
    for _ in range(TOP_K):
        m = logits.max(axis=-1, keepdims=True)
        idx = jnp.min(jnp.where(logits == m, lane, N_EXPERTS), axis=-1, keepdims=True)
        vals.append(m)
        idxs.append(idx)
        logits = jnp.where(lane == idx, -jnp.inf, logits)
    exps = [jnp.exp(v - vals[0]) for v in vals]
    denom = exps[0]
    for e in exps[1:]:
        denom = denom + e
    return idxs, [e / denom for e in exps]


def _norm_mod_kernel(x_ref, w_ref, sh_ref, sc_ref, *rest, tm, with_router):
    g = _group_of_row(pl.program_id(0) * tm)
    y = _rms(x_ref[...], w_ref[...])
    h = y * (1.0 + sc_ref[pl.ds(g, 1), :]) + sh_ref[pl.ds(g, 1), :]
    if not with_router:
        (h_ref,) = rest
        h_ref[...] = h.astype(BF16)
        return
    rw_ref, rb_ref, hp_ref, ti_ref, gt_ref, rk_ref, cnt_ref, seen = rest
    logits = jnp.dot(h, rw_ref[...], preferred_element_type=F32,
                     precision=lax.Precision.HIGHEST) + rb_ref[...]
    idxs, gates = _top_k_gates(logits)

    @pl.when(pl.program_id(0) == 0)
    def _():
        seen[...] = jnp.zeros_like(seen)

    before = (lax.broadcasted_iota(jnp.int32, (tm, tm), 1)
              < lax.broadcasted_iota(jnp.int32, (tm, tm), 0)).astype(BF16)
    lane = lax.broadcasted_iota(jnp.int32, (tm, N_EXPERTS), 1)
    count = seen[...]
    for k in range(TOP_K):
        onehot = (lane == idxs[k]).astype(F32)
        earlier = jnp.dot(before, onehot.astype(BF16), preferred_element_type=F32) + count
        rk_ref[:, k:k + 1] = jnp.sum(onehot * earlier, axis=-1, keepdims=True).astype(jnp.int32)
        count = count + jnp.sum(onehot, axis=0, keepdims=True)
        ti_ref[:, k:k + 1] = idxs[k]
        gt_ref[:, k:k + 1] = gates[k]
    seen[...] = count
    cnt_ref[...] = count.astype(jnp.int32)
    hp_ref[...] = h


def _norm_mod(x, norm_w, mod, sh_chunk, sc_chunk, router_w=None, router_b=None, *, name):
    tm = 256
    with_router = router_w is not None
    in_specs = [pl.BlockSpec((tm, D_MODEL), lambda i: (i, 0)),
                pl.BlockSpec((1, D_MODEL), lambda i: (0, 0)),
                pl.BlockSpec((COND_ROWS, D_MODEL), lambda i: (0, sh_chunk)),
                pl.BlockSpec((COND_ROWS, D_MODEL), lambda i: (0, sc_chunk))]
    args = [x, norm_w.reshape(1, D_MODEL), mod, mod]
    if with_router:
        in_specs += [pl.BlockSpec((D_MODEL, N_EXPERTS), lambda i: (0, 0)),
                     pl.BlockSpec((1, N_EXPERTS), lambda i: (0, 0))]
        args += [router_w, router_b.reshape(1, N_EXPERTS)]
        slot = pl.BlockSpec((tm, TOP_K), lambda i: (i, 0))
        out_specs = [pl.BlockSpec((tm, D_MODEL), lambda i: (i, 0)), slot, slot, slot,
                     pl.BlockSpec((1, N_EXPERTS), lambda i: (0, 0))]
        out_shape = [jax.ShapeDtypeStruct((N_TOK, D_MODEL), F32),
                     jax.ShapeDtypeStruct((N_TOK, TOP_K), jnp.int32),
                     jax.ShapeDtypeStruct((N_TOK, TOP_K), F32),
                     jax.ShapeDtypeStruct((N_TOK, TOP_K), jnp.int32),
                     jax.ShapeDtypeStruct((1, N_EXPERTS), jnp.int32)]
        scratch = [pltpu.VMEM((1, N_EXPERTS), F32)]
    else:
        out_specs = [pl.BlockSpec((tm, D_MODEL), lambda i: (i, 0))]
        out_shape = [jax.ShapeDtypeStruct((N_TOK, D_MODEL), BF16)]
        scratch = []
    out = pl.pallas_call(
        functools.partial(_norm_mod_kernel, tm=tm, with_router=with_router),
        grid=(N_TOK // tm,), in_specs=in_specs, out_specs=out_specs, out_shape=out_shape,
        scratch_shapes=scratch, compiler_params=_cparams(1), name=name)(*args)
    return out if with_router else out[0]


def _rms_kernel(x_ref, w_ref, o_ref):
    o_ref[...] = _rms(x_ref[...], w_ref[...]).astype(o_ref.dtype)


def _rmsnorm_cols(x, w, col_block, width, out_dtype, *, name):
    tm = 256
    m = x.shape[0]
    return pl.pallas_call(
        _rms_kernel, grid=(m // tm,),
        in_specs=[pl.BlockSpec((tm, width), lambda i: (i, col_block)),
                  pl.BlockSpec((1, width), lambda i: (0, 0))],
        out_specs=pl.BlockSpec((tm, width), lambda i: (i, 0)),
        out_shape=jax.ShapeDtypeStruct((m, width), out_dtype),
        compiler_params=_cparams(1), name=name)(x, w.reshape(1, width))


def _rope_tables():
    t = jnp.arange(DEC_SEQ, dtype=jnp.int32)
    row = (t // GRID_W).astype(F32)[:, None]
    col = (t % GRID_W).astype(F32)[:, None]
    half = QK_ROPE // 2
    inv = ROPE_BASE ** (-jnp.arange(0, half, 2, dtype=F32) / half)[None, :]
    ar, ac = row * inv, col * inv
    cos64 = jnp.concatenate([jnp.cos(ar), jnp.cos(ar), jnp.cos(ac), jnp.cos(ac)], axis=-1)
    sin64 = jnp.concatenate([-jnp.sin(ar), jnp.sin(ar), -jnp.sin(ac), jnp.sin(ac)], axis=-1)
    return jnp.tile(cos64, (1, 2)), jnp.tile(sin64, (1, 2))


def _swap16(x):
    lane = lax.broadcasted_iota(jnp.int32, x.shape, x.ndim - 1)
    return jnp.where((lane & 16) == 0,
                     pltpu.roll(x, x.shape[-1] - 16, x.ndim - 1),
                     pltpu.roll(x, 16, x.ndim - 1))


def _rope_kernel(x_ref, cos_ref, sin_ref, o_ref, *, n_chunks):
    cos = cos_ref[...]
    sin = sin_ref[...]
    for c in range(n_chunks):
        x = x_ref[:, c * LANE:(c + 1) * LANE]
        o_ref[:, c * LANE:(c + 1) * LANE] = (x * cos + _swap16(x) * sin).astype(o_ref.dtype)


def _rope_latent(x, cos, sin, col0, width, *, name):
    tm, tc = 256, 2048
    assert width % tc == 0 and col0 % tc == 0
    r0, c0 = N_CTX // tm, col0 // tc
    per_seq = DEC_SEQ // tm
    return pl.pallas_call(
        functools.partial(_rope_kernel, n_chunks=tc // LANE),
        grid=(N_LAT // tm, width // tc),
        in_specs=[pl.BlockSpec((tm, tc), lambda i, j: (r0 + i, c0 + j)),
                  pl.BlockSpec((tm, LANE), lambda i, j: (i % per_seq, 0)),
                  pl.BlockSpec((tm, LANE), lambda i, j: (i % per_seq, 0))],
        out_specs=pl.BlockSpec((tm, tc), lambda i, j: (i, j)),
        out_shape=jax.ShapeDtypeStruct((N_LAT, width), BF16),
        compiler_params=_cparams(2), name=name)(x, cos, sin)


def _rope_kpe_kernel(x_ref, cos_ref, sin_ref, o_ref):
    x = x_ref[...]
    o_ref[...] = x * cos_ref[...] + _swap16(x) * sin_ref[...]


def _rope_kpe(kpe_pad, cos, sin, *, name):
    tm = 256
    per_seq = DEC_SEQ // tm
    return pl.pallas_call(
        _rope_kpe_kernel, grid=(N_LAT // tm,),
        in_specs=[pl.BlockSpec((tm, LANE), lambda i: (i, 0)),
                  pl.BlockSpec((tm, LANE), lambda i: (i % per_seq, 0)),
                  pl.BlockSpec((tm, LANE), lambda i: (i % per_seq, 0))],
        out_specs=pl.BlockSpec((tm, LANE), lambda i: (i, 0)),
        out_shape=jax.ShapeDtypeStruct((N_LAT, LANE), F32),
        compiler_params=_cparams(1), name=name)(kpe_pad, cos, sin)


def _softmax_pv(s_parts, v_parts):
    m = s_parts[0].max(axis=-1, keepdims=True)
    for s in s_parts[1:]:
        m = jnp.maximum(m, s.max(axis=-1, keepdims=True))
    acc, l = None, None
    for s, v in zip(s_parts, v_parts):
        p = jnp.exp(s - m)
        ps = p.sum(axis=-1, keepdims=True)
        pv = jnp.dot(p.astype(BF16), v, preferred_element_type=F32)
        acc = pv if acc is None else acc + pv
        l = ps if l is None else l + ps
    return acc / l


def _diff_lambda(lam_ref, lambda_init):
    lv = lam_ref[...]
    s1 = jnp.sum(lv[0:1] * lv[1:2], axis=-1, keepdims=True)
    s2 = jnp.sum(lv[2:3] * lv[3:4], axis=-1, keepdims=True)
    return jnp.exp(s1) - jnp.exp(s2) + lambda_init


def _diff_finish(o1, o2, lam, subw, lambda_init):
    o = o1 - lam * o2
    return _rms(o, subw) * (1.0 - lambda_init)


def _diff_ctx_kernel(lam_ref, q_ref, k_ref, v_ref, subw_ref, o_ref, ks_ref, vs_ref, *, lambda_init):
    dk = A_QK_DIM
    lam = _diff_lambda(lam_ref, lambda_init)
    for h in range(ATTN_HB):
        cols = slice(h * LANE, (h + 1) * LANE)
        k = k_ref[:, cols]
        v = v_ref[:, cols]
        ks_ref[h] = k
        vs_ref[h] = v
        q = (q_ref[:, cols] * (dk ** -0.5)).astype(BF16)
        kb = k.astype(BF16)
        vb = v.astype(BF16)
        outs = []
        for lo in (0, dk):
            s = lax.dot_general(q[:, lo:lo + dk], kb[:, lo:lo + dk], _NT_DIMS, preferred_element_type=F32)
            outs.append(_softmax_pv([s], [vb]))
        o_ref[:, cols] = _diff_finish(outs[0], outs[1], lam, subw_ref[...], lambda_init).astype(BF16)


def _diff_ctx(proj, lamv, subw, lambda_init):
    w = ATTN_HB * LANE
    kc, vc = A_WIDTH // w, 2 * A_WIDTH // w
    blk = lambda off: pl.BlockSpec((SEQ, w), lambda b, h: (b, off + h))
    st = pl.BlockSpec((None, ATTN_HB, SEQ, LANE), lambda b, h: (b, h, 0, 0))
    return pl.pallas_call(
        functools.partial(_diff_ctx_kernel, lambda_init=lambda_init),
        grid=(BATCH, A_HEADS // ATTN_HB),
        in_specs=[pl.BlockSpec((4, A_QK_DIM), lambda b, h: (0, 0)),
                  blk(0), blk(kc), blk(vc),
                  pl.BlockSpec((1, A_V_DIM), lambda b, h: (0, 0))],
        out_specs=[pl.BlockSpec((SEQ, w), lambda b, h: (b, h)), st, st],
        out_shape=[jax.ShapeDtypeStruct((N_CTX, A_WIDTH), BF16),
                   jax.ShapeDtypeStruct((BATCH, A_HEADS, SEQ, 2 * A_QK_DIM), F32),
                   jax.ShapeDtypeStruct((BATCH, A_HEADS, SEQ, A_V_DIM), F32)],
        compiler_params=_cparams(2), name="diff_attn_ctx")(lamv, proj, proj, proj, subw)


def _diff_lat_kernel(lam_ref, q_ref, k_ref, v_ref, ck_ref, cv_ref, subw_ref, o_ref, *, lambda_init):
    dk = A_QK_DIM
    lam = _diff_lambda(lam_ref, lambda_init)
    for h in range(ATTN_HB):
        cols = slice(h * LANE, (h + 1) * LANE)
        q = q_ref[:, cols] * (dk ** -0.5)
        kb = k_ref[:, cols]
        ckb = ck_ref[h].astype(BF16)
        vb = v_ref[:, cols].astype(BF16)
        cvb = cv_ref[h].astype(BF16)
        outs = []
        for lo in (0, dk):
            qh = q[:, lo:lo + dk]
            sc = lax.dot_general(qh, ckb[:, lo:lo + dk], _NT_DIMS, preferred_element_type=F32)
            sn = lax.dot_general(qh, kb[:, lo:lo + dk], _NT_DIMS, preferred_element_type=F32)
            outs.append(_softmax_pv([sc, sn], [cvb, vb]))
        o_ref[:, cols] = _diff_finish(outs[0], outs[1], lam, subw_ref[...], lambda_init).astype(BF16)


def _diff_lat(qk_rot, proj, cache_k, cache_v, lamv, subw, lambda_init):
    tq = 256
    nq = DEC_SEQ // tq
    w = ATTN_HB * LANE
    kc, vc = A_WIDTH // w, 2 * A_WIDTH // w
    v_row0 = N_CTX // DEC_SEQ
    cache = pl.BlockSpec((None, ATTN_HB, PAST_LEN, LANE), lambda b, h, i: (b, h, 0, 0))
    return pl.pallas_call(
        functools.partial(_diff_lat_kernel, lambda_init=lambda_init),
        grid=(DEC_BATCH, A_HEADS // ATTN_HB, nq),
        in_specs=[pl.BlockSpec((4, A_QK_DIM), lambda b, h, i: (0, 0)),
                  pl.BlockSpec((tq, w), lambda b, h, i: (b * nq + i, h)),
                  pl.BlockSpec((DEC_SEQ, w), lambda b, h, i: (b, kc + h)),
                  pl.BlockSpec((DEC_SEQ, w), lambda b, h, i: (v_row0 + b, vc + h)),
                  cache, cache,
                  pl.BlockSpec((1, A_V_DIM), lambda b, h, i: (0, 0))],
        out_specs=pl.BlockSpec((tq, w), lambda b, h, i: (b * nq + i, h)),
        out_shape=jax.ShapeDtypeStruct((N_LAT, A_WIDTH), BF16),
        compiler_params=_cparams(3), name="diff_attn_lat")(
            lamv, qk_rot, qk_rot, proj, cache_k, cache_v, subw)


def _mla_kernel(qn_ref, qp_ref, kv_ref, kpe_ref, o_ref):
    scale = (QK_NOPE + QK_ROPE) ** -0.5
    kpe = kpe_ref[...].astype(BF16)
    for h in range(ATTN_HB):
        qn = (qn_ref[:, h * LANE:(h + 1) * LANE] * scale).astype(BF16)
        qp = (qp_ref[:, h * QK_ROPE:(h + 1) * QK_ROPE].astype(F32) * scale).astype(BF16)
        kn = kv_ref[:, 2 * h * LANE:(2 * h + 1) * LANE]
        v = kv_ref[:, (2 * h + 1) * LANE:(2 * h + 2) * LANE]
        s = lax.dot_general(qn, kn, _NT_DIMS, preferred_element_type=F32)
        s = s + lax.dot_general(qp, kpe, _NT_DIMS, preferred_element_type=F32)
        o_ref[:, h * LANE:(h + 1) * LANE] = _softmax_pv([s], [v]).astype(BF16)


def _mla_ctx(q_all, kv_ctx, kpe_all):
    hg = C_HEADS // ATTN_HB
    pe0 = C_HEADS * QK_NOPE // (ATTN_HB * QK_ROPE)
    return pl.pallas_call(
        _mla_kernel, grid=(BATCH, hg),
        in_specs=[pl.BlockSpec((SEQ, ATTN_HB * LANE), lambda b, h: (b, h)),
                  pl.BlockSpec((SEQ, ATTN_HB * QK_ROPE), lambda b, h: (b, pe0 + h)),
                  pl.BlockSpec((SEQ, 2 * ATTN_HB * LANE), lambda b, h: (b, h)),
                  pl.BlockSpec((SEQ, QK_ROPE), lambda b, h: (b, 0))],
        out_specs=pl.BlockSpec((SEQ, ATTN_HB * LANE), lambda b, h: (b, h)),
        out_shape=jax.ShapeDtypeStruct((N_CTX, C_HEADS * V_DIM), BF16),
        compiler_params=_cparams(2), name="mla_ctx")(q_all, q_all, kv_ctx, kpe_all)


def _mla_lat(q_all, qpe_rot, kv_lat, kpe_lat):
    tq = 256
    nq = DEC_SEQ // tq
    hg = C_HEADS // ATTN_HB
    lk = PAST_LEN + DEC_SEQ
    r0 = N_CTX // tq
    return pl.pallas_call(
        _mla_kernel, grid=(DEC_BATCH, hg, nq),
        in_specs=[pl.BlockSpec((tq, ATTN_HB * LANE), lambda b, h, i: (r0 + b * nq + i, h)),
                  pl.BlockSpec((tq, ATTN_HB * QK_ROPE), lambda b, h, i: (b * nq + i, h)),
                  pl.BlockSpec((lk, 2 * ATTN_HB * LANE), lambda b, h, i: (b, h)),
                  pl.BlockSpec((None, lk, QK_ROPE), lambda b, h, i: (b, 0, 0))],
        out_specs=pl.BlockSpec((tq, ATTN_HB * LANE), lambda b, h, i: (b * nq + i, h)),
        out_shape=jax.ShapeDtypeStruct((N_LAT, C_HEADS * V_DIM), BF16),
        compiler_params=_cparams(3), name="mla_lat")(q_all, qpe_rot, kv_lat, kpe_lat)


def _conv_kernel(gb_ref, gc_ref, x_ref, cw_ref, cb_ref, z_ref, *, seq):
    u = gc_ref[...] * x_ref[...]
    row = lax.broadcasted_iota(jnp.int32, u.shape, 0)
    prev = jnp.where(row == 0, 0.0, pltpu.roll(u, 1, 0))
    nxt = jnp.where(row == seq - 1, 0.0, pltpu.roll(u, seq - 1, 0))
    cw = cw_ref[...]
    y = prev * cw[0:1] + u * cw[1:2] + nxt * cw[2:3] + cb_ref[...]
    z_ref[...] = (gb_ref[...] * y).astype(BF16)


def _short_conv(proj, conv_w, conv_b, row0, n_rows, seq, *, name):
    tc = 512
    base = 3 * A_WIDTH // tc
    nb = B_WIDTH // tc
    r0 = row0 // seq
    blk = lambda off: pl.BlockSpec((seq, tc), lambda i, j: (r0 + i, base + off * nb + j))
    return pl.pallas_call(
        functools.partial(_conv_kernel, seq=seq),
        grid=(n_rows // seq, nb),
        in_specs=[blk(0), blk(1), blk(2),
                  pl.BlockSpec((3, tc), lambda i, j: (0, j)),
                  pl.BlockSpec((1, tc), lambda i, j: (0, j))],
        out_specs=pl.BlockSpec((seq, tc), lambda i, j: (i, j)),
        out_shape=jax.ShapeDtypeStruct((n_rows, B_WIDTH), BF16),
        compiler_params=_cparams(2), name=name)(proj, proj, proj, conv_w, conv_b.reshape(1, B_WIDTH))


def _moe_kernel(ce_ref, crow_ref, cnt_ref, cidx_ref, xs_hbm, wgu_hbm, bg_ref, bl_ref, wd_ref, bd_ref,
                y_ref, stage, xb, act, wring, sem, wsem, *, nj, ta):
    c = pl.program_id(0)
    s = pl.program_id(1)
    n_tiles = cnt_ref[c]
    valid = n_tiles > 0
    half_rows = R_CAP // 2
    nxt = jnp.minimum(c + 1, N_CHUNKS - 1)
    next_valid = (c + 1 < N_CHUNKS) & (cnt_ref[nxt] > 0)

    def x_copy(ci, half):
        row0 = pl.multiple_of(crow_ref[ci], TM_MOE) + half * half_rows
        return pltpu.make_async_copy(xs_hbm.at[pl.ds(row0, half_rows)], stage, sem)

    def land(half):
        for r in range(half_rows // TM_MOE):
            rows = slice(r * TM_MOE, (r + 1) * TM_MOE)
            xb[half * half_rows + r * TM_MOE:half * half_rows + (r + 1) * TM_MOE, :] = stage[rows, :].astype(BF16)

    @pl.when((c == 0) & (s == 0))
    def _():
        x_copy(0, 0).start()
        x_copy(0, 0).wait()
        land(0)
        x_copy(0, 1).start()

    @pl.when(valid & (s == 0))
    def _():
        x_copy(c, 1).wait()
        land(1)

    @pl.when(valid & next_valid & (s == nj))
    def _():
        x_copy(nxt, 0).start()

    @pl.when(valid & next_valid & (s == nj + 4))
    def _():
        x_copy(nxt, 0).wait()
        land(0)
        x_copy(nxt, 1).start()

    def for_row_count(body):
        lo_tiles = 0
        for m in MOE_ROW_STEPS:
            hi_tiles = m // TM_MOE
            pl.when(valid & (n_tiles > lo_tiles) & (n_tiles <= hi_tiles))(functools.partial(body, m))
            lo_tiles = hi_tiles

    def w_copies(q):
        cq = q // nj
        col0 = pl.multiple_of((q % nj) * ta, ta)
        slot = q % GU_SLOTS
        src = wgu_hbm.at[ce_ref[jnp.minimum(cq, N_CHUNKS - 1)]]
        return [pltpu.make_async_copy(src.at[:, pl.ds(part * D_FF + col0, ta)], wring.at[slot, part],
                                      wsem.at[slot]) for part in range(2)]

    def w_exists(q):
        cq = q // nj
        return (cq < N_CHUNKS) & (cnt_ref[jnp.minimum(cq, N_CHUNKS - 1)] > 0)

    q_now = c * nj + jnp.minimum(s, nj - 1)

    @pl.when((c == 0) & (s == 0))
    def _():
        for q in range(GU_SLOTS - 1):
            for cp in w_copies(q):
                cp.start()

    @pl.when(valid & (s < nj))
    def _():
        @pl.when(w_exists(q_now + GU_SLOTS - 1))
        def _():
            for cp in w_copies(q_now + GU_SLOTS - 1):
                cp.start()

        for cp in w_copies(q_now):
            cp.wait()

    def gate_up(m):
        x = xb[:m, :]
        slot = q_now % GU_SLOTS
        g = jnp.dot(x, wring[slot, 0].astype(BF16), preferred_element_type=F32) + bg_ref[...]
        l = jnp.dot(x, wring[slot, 1].astype(BF16), preferred_element_type=F32) + bl_ref[...]
        glu = jnp.minimum(g, SWIGLU_LIMIT)
        lin = jnp.clip(l, -SWIGLU_LIMIT, SWIGLU_LIMIT)
        col0 = pl.multiple_of(jnp.minimum(s, nj - 1) * ta, ta)
        act[:m, pl.ds(col0, ta)] = (glu * jax.nn.sigmoid(SWIGLU_ALPHA * glu) * (lin + 1.0)).astype(BF16)

    def down(m):
        y_ref[:m, :] = jnp.dot(act[:m, :], wd_ref[...].astype(BF16), preferred_element_type=F32) + bd_ref[...]
        if m < R_CAP:
            y_ref[m:, :] = jnp.zeros((R_CAP - m, 2 * ta), F32)

    @pl.when(s < nj)
    def _():
        for_row_count(gate_up)

    @pl.when(s >= nj)
    def _():
        for_row_count(down)


def _moe_experts(xs, chunk_e, chunk_row, chunk_tiles, chunk_idx, w_gu, b_gu, w_down, b_down):
    ta = 256
    nj = D_FF // ta
    nd = D_MODEL // (2 * ta)

    def gu_j(c, s, cnt):
        return jnp.where(cnt[c] > 0, jnp.minimum(s, nj - 1), nj - 1)

    def dn_j(c, s, cnt):
        return jnp.where(cnt[c] > 0, jnp.maximum(s - nj, 0), nd - 1)

    return pl.pallas_call(
        functools.partial(_moe_kernel, nj=nj, ta=ta),
        grid_spec=pltpu.PrefetchScalarGridSpec(
            num_scalar_prefetch=4, grid=(N_CHUNKS, nj + nd),
            in_specs=[
                pl.BlockSpec(memory_space=pl.ANY),
                pl.BlockSpec(memory_space=pl.ANY),
                pl.BlockSpec((None, 1, ta), lambda c, s, ce, cr, cnt, ci: (ce[c], 0, gu_j(c, s, cnt))),
                pl.BlockSpec((None, 1, ta), lambda c, s, ce, cr, cnt, ci: (ce[c], 0, nj + gu_j(c, s, cnt))),
                pl.BlockSpec((None, D_FF, 2 * ta), lambda c, s, ce, cr, cnt, ci: (ce[c], 0, dn_j(c, s, cnt))),
                pl.BlockSpec((None, 1, 2 * ta), lambda c, s, ce, cr, cnt, ci: (ce[c], 0, dn_j(c, s, cnt))),
            ],
            out_specs=pl.BlockSpec((R_CAP, 2 * ta), lambda c, s, ce, cr, cnt, ci: (ci[c], dn_j(c, s, cnt))),
            scratch_shapes=[pltpu.VMEM((R_CAP // 2, D_MODEL), F32),
                            pltpu.VMEM((R_CAP, D_MODEL), BF16),
                            pltpu.VMEM((R_CAP, D_FF), BF16),
                            pltpu.VMEM((GU_SLOTS, 2, D_MODEL, ta), F32),
                            pltpu.SemaphoreType.DMA(()),
                            pltpu.SemaphoreType.DMA((GU_SLOTS,))]),
        out_shape=jax.ShapeDtypeStruct((PY_MOE, D_MODEL), F32),
        compiler_params=pltpu.CompilerParams(dimension_semantics=("arbitrary", "arbitrary"),
                                             vmem_limit_bytes=MOE_VMEM_LIMIT),
        name="moe_experts")(
            chunk_e, chunk_row, chunk_tiles, chunk_idx, xs, w_gu,
            b_gu.reshape(N_EXPERTS, 1, 2 * D_FF), b_gu.reshape(N_EXPERTS, 1, 2 * D_FF),
            w_down, b_down.reshape(N_EXPERTS, 1, D_MODEL))


def _moe_routing(top_i, rank, counts):
    e_flat = top_i.reshape(-1)
    rank = rank.reshape(-1)
    counts = counts.reshape(-1)
    padded = ((counts + TM_MOE - 1) // TM_MOE) * TM_MOE
    starts = jnp.cumsum(padded) - padded
    n_ch = (counts + R_CAP - 1) // R_CAP
    ch_end = jnp.cumsum(n_ch)
    ch_base = ch_end - n_ch
    n_chunks = ch_end[-1]
    pos_x = starts[e_flat] + rank
    pos_y = (ch_base[e_flat] + rank // R_CAP) * R_CAP + rank % R_CAP
    row_token = jnp.zeros((PX_MOE,), jnp.int32).at[pos_x].set(
        jnp.arange(N_TOK * TOP_K, dtype=jnp.int32) // TOP_K)
    cid = jnp.arange(N_CHUNKS, dtype=jnp.int32)
    cvalid = cid < n_chunks
    cidx = jnp.minimum(cid, n_chunks - 1)
    chunk_e = jnp.sum((ch_end[None, :] <= cidx[:, None]).astype(jnp.int32), axis=1)
    chunk_e = jnp.minimum(chunk_e, N_EXPERTS - 1)
    k = cidx - ch_base[chunk_e]
    chunk_row = starts[chunk_e] + k * R_CAP
    rows_left = jnp.clip(counts[chunk_e] - k * R_CAP, 0, R_CAP)
    chunk_tiles = jnp.where(cvalid, (rows_left + TM_MOE - 1) // TM_MOE, 0).astype(jnp.int32)
    return pos_y.reshape(N_TOK, TOP_K), row_token, chunk_e, chunk_row.astype(jnp.int32), chunk_tiles, cidx


def _moe_combine_kernel(x_ref, y0_ref, y1_ref, y2_ref, y3_ref, gt_ref, g_ref, o_ref, *, tm):
    g = _group_of_row(pl.program_id(0) * tm)
    gates = gt_ref[...]
    acc = None
    for k, y_ref in enumerate((y0_ref, y1_ref, y2_ref, y3_ref)):
        term = gates[:, k:k + 1] * y_ref[...]
        acc = term if acc is None else acc + term
    o_ref[...] = x_ref[...] + g_ref[pl.ds(g, 1), :] * acc


def _moe_combine(x, yg, gates, mod, chunk, *, name):
    tm = 128
    nb = N_TOK // tm
    y_spec = lambda k: pl.BlockSpec((tm, D_MODEL), lambda i: (k * nb + i, 0))
    return pl.pallas_call(
        functools.partial(_moe_combine_kernel, tm=tm), grid=(nb,),
        in_specs=[pl.BlockSpec((tm, D_MODEL), lambda i: (i, 0)),
                  y_spec(0), y_spec(1), y_spec(2), y_spec(3),
                  pl.BlockSpec((tm, TOP_K), lambda i: (i, 0)),
                  pl.BlockSpec((COND_ROWS, D_MODEL), lambda i: (0, chunk))],
        out_specs=pl.BlockSpec((tm, D_MODEL), lambda i: (i, 0)),
        out_shape=jax.ShapeDtypeStruct((N_TOK, D_MODEL), F32),
        compiler_params=_cparams(1), name=name)(x, yg, yg, yg, yg, gates, mod)


def _moe(x, routed, mod, w_gu, b_gu, w_down, b_down, *, name):
    h, top_i, gates, rank, counts = routed
    pos_y, row_token, chunk_e, chunk_row, chunk_tiles, chunk_idx = _moe_routing(top_i, rank, counts)
    xs = h.at[row_token].get(mode="promise_in_bounds")
    y = _moe_experts(xs, chunk_e, chunk_row, chunk_tiles, chunk_idx, w_gu, b_gu, w_down, b_down)
    yg = y.at[pos_y.T.reshape(-1)].get(mode="promise_in_bounds")
    return _moe_combine(x, yg, gates, mod, 5, name=name)


def _lambda_init(layer):
    return 0.8 - 0.6 * math.exp(-0.3 * layer)


def kernel(x_prompt, x_sample, cache_l0_k, cache_l0_v, cache_l1_ckv, cache_l1_kpe, c, c_ctx, mod0_w, mod0_b, norm0a_w, in0_w, conv0_w, conv0_b, lam0_q1, lam0_k1, lam0_q2, lam0_k2, subln0_w, out0_w, norm0b_w, moe0_router_w, moe0_router_b, moe0_w_gu, moe0_b_gu, moe0_w_down, moe0_b_down, mod1_w, mod1_b, norm1a_w, dqkv1_w, qnorm1_w, kvnorm1_w, uq1_w, ukv1_w, o1_w, norm1b_w, moe1_router_w, moe1_router_b, moe1_w_gu, moe1_b_gu, moe1_w_down, moe1_b_down, final_norm_w):
    x = jnp.concatenate([x_prompt.reshape(N_CTX, D_MODEL), x_sample.reshape(N_LAT, D_MODEL)], axis=0)
    cond = jnp.zeros((COND_ROWS, D_MODEL), F32).at[0].set(c_ctx).at[1:1 + DEC_BATCH].set(c)
    scond = jax.nn.silu(cond).astype(BF16)
    cos, sin = _rope_tables()

    mod = _mm(scond, mod0_w, tm=COND_ROWS, tn=512, out_dtype=F32,
              bias=mod0_b.reshape(1, -1), name="mod0")
    h = _norm_mod(x, norm0a_w, mod, 0, 1, name="norm0a")
    proj = _mm(h, in0_w, tm=1024, tn=512, out_dtype=F32, name="in0_proj")
    qk_rot = _rope_latent(proj, cos, sin, 0, 2 * A_WIDTH, name="rope0")
    lamv = jnp.stack([lam0_q1, lam0_k1, lam0_q2, lam0_k2])
    subw = subln0_w.reshape(1, A_V_DIM)
    li = _lambda_init(0)
    o_ctx, state_l0_k, state_l0_v = _diff_ctx(proj, lamv, subw, li)
    o_lat = _diff_lat(qk_rot, proj, cache_l0_k, cache_l0_v, lamv, subw, li)
    z_ctx = _short_conv(proj, conv0_w, conv0_b, 0, N_CTX, SEQ, name="conv_ctx")
    z_lat = _short_conv(proj, conv0_w, conv0_b, N_CTX, N_LAT, DEC_SEQ, name="conv_lat")
    mix = jnp.concatenate([jnp.concatenate([o_ctx, o_lat], axis=0),
                           jnp.concatenate([z_ctx, z_lat], axis=0)], axis=1)
    x = _mm(mix, out0_w, tm=1024, tn=512, out_dtype=F32, resid=x, gate=mod, gate_chunk=2, name="out0_proj")
    routed = _norm_mod(x, norm0b_w, mod, 3, 4, moe0_router_w, moe0_router_b, name="norm0b")
    x = _moe(x, routed, mod, moe0_w_gu, moe0_b_gu, moe0_w_down, moe0_b_down, name="moe0_combine")

    mod = _mm(scond, mod1_w, tm=COND_ROWS, tn=512, out_dtype=F32,
              bias=mod1_b.reshape(1, -1), name="mod1")
    h = _norm_mod(x, norm1a_w, mod, 0, 1, name="norm1a")
    proj = _mm(h, dqkv1_w, tm=1024, tn=256, out_dtype=F32, name="dqkv_proj")
    cq = _rmsnorm_cols(proj, qnorm1_w, 0, Q_LORA, BF16, name="q_norm")
    ckv = _rmsnorm_cols(proj, kvnorm1_w, Q_LORA // KV_LORA, KV_LORA, F32, name="kv_norm")
    kpe = proj[:, Q_LORA + KV_LORA:]
    uq = uq1_w.reshape(Q_LORA, C_HEADS, QK_NOPE + QK_ROPE)
    uq = jnp.concatenate([uq[:, :, :QK_NOPE].reshape(Q_LORA, -1), uq[:, :, QK_NOPE:].reshape(Q_LORA, -1)], axis=1)
    q_all = _mm(cq, uq, tm=1024, tn=512, out_dtype=F32, name="uq_proj")
    qpe_rot = _rope_latent(q_all, cos, sin, C_HEADS * QK_NOPE, C_HEADS * QK_ROPE, name="rope1_q")
    kpe_lat = kpe[N_CTX:]
    kpe_rot = _rope_kpe(jnp.concatenate([kpe_lat, kpe_lat], axis=1), cos, sin, name="rope1_k")[:, :QK_ROPE]
    kpe_lat_all = jnp.concatenate([cache_l1_kpe, kpe_rot.reshape(DEC_BATCH, DEC_SEQ, QK_ROPE)], axis=1)
    ckv_lat_all = jnp.concatenate([cache_l1_ckv, ckv[N_CTX:].reshape(DEC_BATCH, DEC_SEQ, KV_LORA)], axis=1)
    kv_ctx = _mm(ckv[:N_CTX].astype(BF16), ukv1_w, tm=1024, tn=512, out_dtype=BF16, name="ukv_ctx")
    kv_lat = _mm(ckv_lat_all.reshape(-1, KV_LORA).astype(BF16), ukv1_w, tm=1280, tn=512,
                 out_dtype=BF16, name="ukv_lat")
    o_ctx = _mla_ctx(q_all, kv_ctx, kpe)
    o_lat = _mla_lat(q_all, qpe_rot, kv_lat, kpe_lat_all)
    mix = jnp.concatenate([o_ctx, o_lat], axis=0)
    x = _mm(mix, o1_w, tm=1024, tn=512, out_dtype=F32, resid=x, gate=mod, gate_chunk=2, name="o1_proj")
    routed = _norm_mod(x, norm1b_w, mod, 3, 4, moe1_router_w, moe1_router_b, name="norm1b")
    x = _moe(x, routed, mod, moe1_w_gu, moe1_b_gu, moe1_w_down, moe1_b_down, name="moe1_combine")

    y = _rmsnorm_cols(x, final_norm_w, 0, D_MODEL, F32, name="final_norm")
    state_l1_ckv = ckv[:N_CTX].reshape(BATCH, SEQ, KV_LORA)
    state_l1_kpe = kpe[:N_CTX].reshape(BATCH, SEQ, QK_ROPE)
    return (y[:N_CTX].reshape(BATCH, SEQ, D_MODEL), y[N_CTX:].reshape(DEC_BATCH, DEC_SEQ, D_MODEL),
            state_l0_k, state_l0_v, state_l1_ckv, state_l1_kpe)
```
